```python
import math
import jax, jax.numpy as jnp
from jax import lax
import numpy as np

D_MODEL = 2048
BATCH = 2
SEQ = 8192
DEPTH = 4

SB_HEADS = 8
SB_HEAD_DIM = 128
SB_WIDTH = SB_HEADS * SB_HEAD_DIM
Q_BLOCK = 128
SSM_HEAD_DIM = 64
SSM_INNER = D_MODEL // 2
SSM_HEADS = SSM_INNER // SSM_HEAD_DIM
SSM_GROUPS = 2
SSM_STATE = 128
SSM_CONV = 4
SSM_CHUNK = 128
CONV_DIM = SSM_INNER + 2 * SSM_GROUPS * SSM_STATE
D_IN_EVEN = 3 * SB_WIDTH + SSM_INNER + CONV_DIM + SSM_HEADS
MIX_WIDTH = SB_WIDTH + SSM_INNER
DIL_HEAD_DIM = 128
DIL_WIDTH = D_MODEL
DIL_HEADS = DIL_WIDTH // DIL_HEAD_DIM
DIL_PATTERNS = ((128, 1), (512, 4), (2048, 16))
DIL_BLOCK = 128
D_FF = 5632
FFN_CONV = 3
EPS = 1e-6
N_EVEN = (DEPTH + 1) // 2
N_ODD = DEPTH // 2

kernel_name = 'hybrid_stickbreak_ssd_dilated_convffn'


def rms_norm(x, w):
    xf = x.astype(jnp.float32)
    y = xf * lax.rsqrt(jnp.mean(xf * xf, axis=-1, keepdims=True) + EPS)
    return (y * w.astype(jnp.float32)).astype(x.dtype)


def causal_dwconv(x, w, b):
    K, C = w.shape
    y = lax.conv_general_dilated(x, w[:, None, :].astype(x.dtype), window_strides=(1,),
                                 padding=[(K - 1, 0)], dimension_numbers=('NWC', 'WIO', 'NWC'),
                                 feature_group_count=C)
    return y + b.astype(x.dtype)


def stick_breaking_attention(q, k, v):
    Bsz, T, H, Dh = q.shape
    nb = T // Q_BLOCK
    scale = Dh ** -0.5
    qb = q.reshape(Bsz, nb, Q_BLOCK, H, Dh).transpose(1, 0, 2, 3, 4)
    key_pos = jnp.arange(T)

    def block(args):
        qi, i = args
        z = jnp.einsum('bqhd,bkhd->bhqk', qi, k) * scale
        q_pos = i * Q_BLOCK + jnp.arange(Q_BLOCK)
        mask = key_pos[None, :] < q_pos[:, None]
        log_beta = jax.nn.log_sigmoid(z)
        log_keep = jnp.where(mask, log_beta - z, 0.0)
        later = lax.cumsum(log_keep, axis=3, reverse=True) - log_keep
        w = jnp.where(mask, jnp.exp(log_beta + later), 0.0)
        return jnp.einsum('bhqk,bkhd->bqhd', w, v)

    out = lax.map(block, (qb, jnp.arange(nb)))
    return out.transpose(1, 0, 2, 3, 4).reshape(Bsz, T, H, Dh)


def ssd_scan(x, dt, a, b_in, c_in):
    Bsz, T, H, P = x.shape
    G, N = b_in.shape[2], b_in.shape[3]
    R = H // G
    L = SSM_CHUNK
    C = T // L
    xdt = (x * dt[..., None]).reshape(Bsz, C, L, G, R, P)
    a_cs = jnp.cumsum((dt * a).reshape(Bsz, C, L, G, R), axis=2)
    bc = b_in.reshape(Bsz, C, L, G, N)
    cc = c_in.reshape(Bsz, C, L, G, N)
    seg = a_cs[:, :, :, None] - a_cs[:, :, None]
    causal = jnp.tril(jnp.ones((L, L), bool))[:, :, None, None]
    decay = jnp.exp(jnp.where(causal, seg, -jnp.inf))
    scores = jnp.einsum('bclgn,bcsgn->bclsg', cc, bc)
    y_diag = jnp.einsum('bclsgr,bcsgrp->bclgrp', scores[..., None] * decay, xdt)
    decay_to_end = jnp.exp(a_cs[:, :, -1:] - a_cs)
    states = jnp.einsum('bclgn,bclgr,bclgrp->bcgrpn', bc, decay_to_end, xdt)
    chunk_decay = jnp.exp(a_cs[:, :, -1])

    def step(h, inp):
        s_c, d_c = inp
        return h * d_c[..., None, None] + s_c, h

    h0 = jnp.zeros((Bsz, G, R, P, N), jnp.float32)
    _, prev = lax.scan(step, h0, (jnp.moveaxis(states, 1, 0), jnp.moveaxis(chunk_decay, 1, 0)))
    prev = jnp.moveaxis(prev, 0, 1)
    y_off = jnp.einsum('bclgn,bcgrpn,bclgr->bclgrp', cc, prev, jnp.exp(a_cs))
    return (y_diag + y_off).reshape(Bsz, T, H, P)


def even_mixer(hn, w_in, conv_w, conv_b, dt_bias, a_log, d_skip, ssm_norm_w, w_out):
    Bsz, T, _ = hn.shape
    f32 = jnp.float32
    proj = hn @ w_in
    cuts = [SB_WIDTH, 2 * SB_WIDTH, 3 * SB_WIDTH, 3 * SB_WIDTH + SSM_INNER,
            3 * SB_WIDTH + SSM_INNER + CONV_DIM]
    q, k, v, z, xbc, dt = jnp.split(proj, cuts, axis=-1)
    heads = lambda t: t.astype(f32).reshape(Bsz, T, SB_HEADS, SB_HEAD_DIM)
    o_a = stick_breaking_attention(heads(q), heads(k), heads(v)).reshape(Bsz, T, SB_WIDTH)
    xbc = jax.nn.silu(causal_dwconv(xbc, conv_w, conv_b)).astype(f32)
    xs, b_in, c_in = jnp.split(xbc, [SSM_INNER, SSM_INNER + SSM_GROUPS * SSM_STATE], axis=-1)
    dt = jax.nn.softplus(dt.astype(f32) + dt_bias.astype(f32))
    a = -jnp.exp(a_log.astype(f32))
    xs = xs.reshape(Bsz, T, SSM_HEADS, SSM_HEAD_DIM)
    y = ssd_scan(xs, dt, a, b_in.reshape(Bsz, T, SSM_GROUPS, SSM_STATE),
                 c_in.reshape(Bsz, T, SSM_GROUPS, SSM_STATE))
    y = y + d_skip.astype(f32)[:, None] * xs
    y = y.reshape(Bsz, T, SSM_INNER) * jax.nn.silu(z.astype(f32))
    o_b = rms_norm(y, ssm_norm_w)
    o = jnp.concatenate([o_a, o_b], axis=-1).astype(hn.dtype)
    return o @ w_out


def dilated_branch(q, k, v, window, dilation):
    Bsz, T, H, Dh = q.shape
    span = dilation * DIL_BLOCK
    T_pad = -(-T // span) * span
    M = T_pad // dilation
    nb = M // DIL_BLOCK
    reach = window // dilation

    def to_sub(t):
        t = jnp.pad(t, ((0, 0), (0, T_pad - T), (0, 0), (0, 0)))
        t = t.reshape(Bsz, M, dilation, H, Dh).transpose(0, 2, 1, 3, 4)
        return t.reshape(Bsz, dilation, nb, DIL_BLOCK, H, Dh)

    def with_prev(t):
        prev = jnp.pad(t, ((0, 0), (0, 0), (1, 0), (0, 0), (0, 0), (0, 0)))[:, :, :-1]
        return jnp.concatenate([prev, t], axis=3)

    def from_sub(t):
        rest = t.shape[4:]
        t = jnp.moveaxis(t.reshape((Bsz, dilation, M) + rest), 1, 2)
        return t.reshape((Bsz, T_pad) + rest)[:, :T]

    qs = to_sub(q)
    kb, vb = with_prev(to_sub(k)), with_prev(to_sub(v))
    s = jnp.einsum('brnqhe,brnkhe->brnhqk', qs, kb) * (Dh ** -0.5)
    qi = jnp.arange(DIL_BLOCK)[:, None]
    kj = jnp.arange(2 * DIL_BLOCK)[None, :]
    dist = qi + DIL_BLOCK - kj
    band = (dist >= 0) & (dist <= reach)
    mask = band[None] & ((jnp.arange(nb)[:, None, None] > 0) | (kj >= DIL_BLOCK)[None])
    s = jnp.where(mask[None, None, :, None], s, -jnp.inf)
    m = jnp.max(s, axis=-1, keepdims=True)
    p = jnp.exp(s - m)
    den = jnp.sum(p, axis=-1)
    o = jnp.einsum('brnhqk,brnkhe->brnqhe', p, vb) / jnp.swapaxes(den, -1, -2)[..., None]
    lse = jnp.swapaxes(m[..., 0] + jnp.log(den), -1, -2)
    return from_sub(o), from_sub(lse)


def dilated_mixture_attention(q, k, v):
    outs, lses = zip(*[dilated_branch(q, k, v, w, d) for (w, d) in DIL_PATTERNS])
    wts = jax.nn.softmax(jnp.stack(lses, 0), axis=0)
    return jnp.einsum('gbth,gbthe->bthe', wts, jnp.stack(outs, 0))


def odd_mixer(hn, w_in, w_out):
    Bsz, T, _ = hn.shape
    proj = (hn @ w_in).astype(jnp.float32)
    q, k, v = [t.reshape(Bsz, T, DIL_HEADS, DIL_HEAD_DIM) for t in jnp.split(proj, 3, axis=-1)]
    o = dilated_mixture_attention(q, k, v).reshape(Bsz, T, DIL_WIDTH).astype(hn.dtype)
    return o @ w_out


def conv_ffn(hn, w_gate, w_up, conv_w, conv_b, w_down):
    g = causal_dwconv(hn @ w_gate, conv_w, conv_b)
    return (jax.nn.silu(g) * (hn @ w_up)) @ w_down


def setup_inputs(seed: int = 0) -> dict:
    key = jax.random.key(seed)
    ks = iter(jax.random.split(key, 32))
    nrm = lambda shape, scale: jax.random.normal(next(ks), shape, jnp.float32) * scale
    gain = lambda shape: 1.0 + 0.02 * jax.random.normal(next(ks), shape, jnp.float32)
    dt0 = jnp.exp(jax.random.uniform(next(ks), (N_EVEN, SSM_HEADS), jnp.float32,
                                     math.log(1e-3), math.log(1e-1)))
    return {
        'x': nrm((BATCH, SEQ, D_MODEL), 1.0),
        'mix_norm_w': gain((DEPTH, D_MODEL)),
        'ffn_norm_w': gain((DEPTH, D_MODEL)),
        'final_norm_w': gain((D_MODEL,)),
        'ev_w_in': nrm((N_EVEN, D_MODEL, D_IN_EVEN), D_MODEL ** -0.5),
        'ev_conv_w': nrm((N_EVEN, SSM_CONV, CONV_DIM), SSM_CONV ** -0.5),
        'ev_conv_b': nrm((N_EVEN, CONV_DIM), 0.01),
        'ev_dt_bias': dt0 + jnp.log(-jnp.expm1(-dt0)),
        'ev_a_log': jnp.log(jax.random.uniform(next(ks), (N_EVEN, SSM_HEADS), jnp.float32, 1.0, 16.0)),
        'ev_d_skip': 1.0 + 0.1 * jax.random.normal(next(ks), (N_EVEN, SSM_HEADS), jnp.float32),
        'ev_ssm_norm_w': gain((N_EVEN, SSM_INNER)),
        'ev_w_out': nrm((N_EVEN, MIX_WIDTH, D_MODEL), MIX_WIDTH ** -0.5),
        'od_w_in': nrm((N_ODD, D_MODEL, 3 * DIL_WIDTH), D_MODEL ** -0.5),
        'od_w_out': nrm((N_ODD, DIL_WIDTH, D_MODEL), DIL_WIDTH ** -0.5),
        'ffn_w_gate': nrm((DEPTH, D_MODEL, D_FF), D_MODEL ** -0.5),
        'ffn_w_up': nrm((DEPTH, D_MODEL, D_FF), D_MODEL ** -0.5),
        'ffn_conv_w': nrm((DEPTH, FFN_CONV, D_FF), FFN_CONV ** -0.5),
        'ffn_conv_b': nrm((DEPTH, D_FF), 0.01),
        'ffn_w_down': nrm((DEPTH, D_FF, D_MODEL), D_FF ** -0.5),
    }


def reference(x, mix_norm_w, ffn_norm_w, final_norm_w, ev_w_in, ev_conv_w, ev_conv_b,
              ev_dt_bias, ev_a_log, ev_d_skip, ev_ssm_norm_w, ev_w_out, od_w_in, od_w_out,
              ffn_w_gate, ffn_w_up, ffn_conv_w, ffn_conv_b, ffn_w_down):
    h = x
    for layer in range(DEPTH):
        hn = rms_norm(h, mix_norm_w[layer])
        i = layer // 2
        if layer % 2 == 0:
            mix = even_mixer(hn, ev_w_in[i], ev_conv_w[i], ev_conv_b[i], ev_dt_bias[i],
                             ev_a_log[i], ev_d_skip[i], ev_ssm_norm_w[i], ev_w_out[i])
        else:
            mix = odd_mixer(hn, od_w_in[i], od_w_out[i])
        h = h + mix.astype(h.dtype)
        f = conv_ffn(rms_norm(h, ffn_norm_w[layer]), ffn_w_gate[layer], ffn_w_up[layer],
                     ffn_conv_w[layer], ffn_conv_b[layer], ffn_w_down[layer])
        h = h + f.astype(h.dtype)
    return rms_norm(h, final_norm_w)
```

```python
import functools

import jax
import jax.numpy as jnp
from jax import lax
from jax.experimental import pallas as pl
from jax.experimental.pallas import tpu as pltpu

F32 = jnp.float32
BF16 = jnp.bfloat16

D_MODEL = 2048
DEPTH = 4
SB_HEADS = 8
HEAD_DIM = 128
SB_WIDTH = SB_HEADS * HEAD_DIM
SSM_HEAD_DIM = 64
SSM_INNER = 1024
SSM_HEADS = 16
SSM_GROUPS = 2
SSM_STATE = 128
SSM_CONV = 4
SSM_CHUNK = 128
CONV_DIM = SSM_INNER + 2 * SSM_GROUPS * SSM_STATE
DIL_HEADS = 16
DIL_PATTERNS = ((128, 1), (512, 4), (2048, 16))
DIL_BLOCK = 128
D_FF = 5632
FFN_CONV = 3
EPS = 1e-6

LANES = 128
SUBLANES = 8
VMEM_LIMIT = 56 * 1024 * 1024
DT_PAD = LANES
ZXD_WIDTH = SSM_INNER + CONV_DIM + DT_PAD

SB_EXIT = 110.0
NEG_BIG = -1e30


def _params(sem, vmem=VMEM_LIMIT):
    return pltpu.CompilerParams(dimension_semantics=sem, vmem_limit_bytes=vmem)


def _rms(x, w):
    ms = jnp.mean(x * x, axis=-1, keepdims=True)
    return x * lax.rsqrt(ms + EPS) * w


def _sigmoid(x):
    return 1.0 / (1.0 + jnp.exp(-x))


def _shift_rows(x, halo, s):
    n = x.shape[0]
    r = pltpu.roll(x, s, axis=0)
    hr = pltpu.roll(halo, s, axis=0)
    rid = lax.broadcasted_iota(jnp.int32, hr.shape, 0)
    top = jnp.where(rid < s, hr, r[:SUBLANES])
    return jnp.concatenate([top, r[SUBLANES:]], axis=0) if n > SUBLANES else top


def _norm_matmul_kernel(x_ref, nw_ref, w_ref, o_ref, xn_ref):
    @pl.when(pl.program_id(1) == 0)
    def _():
        xn_ref[...] = _rms(x_ref[...], nw_ref[...]).astype(BF16)

    o_ref[...] = jnp.dot(xn_ref[...], w_ref[...], preferred_element_type=F32).astype(o_ref.dtype)


def norm_matmul(x, nw, w, out_dtype, tm=1024, tn=512):
    m, d = x.shape
    n = w.shape[1]
    tm, tn = min(tm, m), min(tn, n)
    assert m % tm == 0 and n % tn == 0
    return pl.pallas_call(
        _norm_matmul_kernel,
        grid=(m // tm, n // tn),
        in_specs=[
            pl.BlockSpec((tm, d), lambda i, j: (i, 0)),
            pl.BlockSpec((1, d), lambda i, j: (0, 0)),
            pl.BlockSpec((d, tn), lambda i, j: (0, j)),
        ],
        out_specs=pl.BlockSpec((tm, tn), lambda i, j: (i, j)),
        out_shape=jax.ShapeDtypeStruct((m, n), out_dtype),
        scratch_shapes=[pltpu.VMEM((tm, d), BF16)],
        compiler_params=_params(("parallel", "arbitrary")),
        name="norm_matmul",
    )(x, nw.reshape(1, d), w)


def _matmul_residual_kernel(*refs, n_in):
    xs, ws, h_ref, o_ref = refs[:n_in], refs[n_in:2 * n_in], refs[2 * n_in], refs[2 * n_in + 1]
    acc = h_ref[...]
    for x_ref, w_ref in zip(xs, ws):
        acc = acc + jnp.dot(x_ref[...], w_ref[...], preferred_element_type=F32)
    o_ref[...] = acc


def matmul_residual(xs, ws, h, tm=512):
    m, n = h.shape
    tm = min(tm, m)
    assert m % tm == 0
    in_specs = [pl.BlockSpec((tm, x.shape[1]), lambda i: (i, 0)) for x in xs]
    in_specs += [pl.BlockSpec(w.shape, lambda i: (0, 0)) for w in ws]
    in_specs += [pl.BlockSpec((tm, n), lambda i: (i, 0))]
    return pl.pallas_call(
        functools.partial(_matmul_residual_kernel, n_in=len(xs)),
        grid=(m // tm,),
        in_specs=in_specs,
        out_specs=pl.BlockSpec((tm, n), lambda i: (i, 0)),
        out_shape=jax.ShapeDtypeStruct((m, n), F32),
        compiler_params=_params(("parallel",)),
        name="matmul_residual",
    )(*xs, *ws, h)


def _conv_ffn_kernel(h_ref, halo_ref, nw_ref, wg_ref, wu_ref, cw_ref, cb_ref, wd_ref, o_ref,
                     xn_ref, xh_ref, *, tiles_per_seq):
    i, f = pl.program_id(0), pl.program_id(1)

    @pl.when(f == 0)
    def _():
        xn_ref[...] = _rms(h_ref[...], nw_ref[...]).astype(BF16)
        seq_start = (i % tiles_per_seq) == 0
        xh = _rms(halo_ref[...], nw_ref[...])
        xh_ref[...] = jnp.where(seq_start, 0.0, xh).astype(BF16)

    xn = xn_ref[...]
    g = jnp.dot(xn, wg_ref[...], preferred_element_type=F32)
    gh = jnp.dot(xh_ref[...], wg_ref[...], preferred_element_type=F32)
    u = jnp.dot(xn, wu_ref[...], preferred_element_type=F32)
    cw = cw_ref[...]
    c = cw[2:3] * g + cw[1:2] * _shift_rows(g, gh, 1) + cw[0:1] * _shift_rows(g, gh, 2) + cb_ref[...]
    act = (c * _sigmoid(c) * u).astype(BF16)
    contrib = jnp.dot(act, wd_ref[...], preferred_element_type=F32)

    @pl.when(f == 0)
    def _():
        o_ref[...] = h_ref[...] + contrib

    @pl.when(f != 0)
    def _():
        o_ref[...] += contrib


def conv_ffn(h, seq_len, nw, wg, wu, cw, cb, wd, tm=512, tf=512):
    m, d = h.shape
    dff = wg.shape[1]
    tm = min(tm, seq_len)
    assert seq_len % tm == 0 and dff % tf == 0 and m % seq_len == 0
    hb = tm // SUBLANES
    return pl.pallas_call(
        functools.partial(_conv_ffn_kernel, tiles_per_seq=seq_len // tm),
        grid=(m // tm, dff // tf),
        in_specs=[
            pl.BlockSpec((tm, d), lambda i, f: (i, 0)),
            pl.BlockSpec((SUBLANES, d), lambda i, f: (jnp.maximum(i * hb - 1, 0), 0)),
            pl.BlockSpec((1, d), lambda i, f: (0, 0)),
            pl.BlockSpec((d, tf), lambda i, f: (0, f)),
            pl.BlockSpec((d, tf), lambda i, f: (0, f)),
            pl.BlockSpec((FFN_CONV, tf), lambda i, f: (0, f)),
            pl.BlockSpec((1, tf), lambda i, f: (0, f)),
            pl.BlockSpec((tf, d), lambda i, f: (f, 0)),
        ],
        out_specs=pl.BlockSpec((tm, d), lambda i, f: (i, 0)),
        out_shape=jax.ShapeDtypeStruct((m, d), F32),
        scratch_shapes=[pltpu.VMEM((tm, d), BF16), pltpu.VMEM((SUBLANES, d), BF16)],
        compiler_params=_params(("parallel", "arbitrary")),
        name="conv_ffn",
    )(h, h, nw.reshape(1, d), wg, wu, cw, cb.reshape(1, dff), wd)


def _sb_kernel(q_ref, k_ref, v_ref, o_ref, *, tile):
    qi = pl.program_id(2)
    q = q_ref[0]
    row = lax.broadcasted_iota(jnp.int32, (tile, tile), 0)
    col = lax.broadcasted_iota(jnp.int32, (tile, tile), 1)
    later = (row > col).astype(BF16)
    strictly_before = col < row

    def key_tile(j, diagonal, run, acc):
        start = pl.multiple_of(j * tile, tile)
        k = k_ref[0, pl.ds(start, tile), :]
        v = v_ref[0, pl.ds(start, tile), :]
        z = lax.dot_general(q, k, (((1,), (1,)), ((), ())), preferred_element_type=F32)
        sp = jnp.maximum(z, 0.0) + jnp.log1p(jnp.exp(-jnp.abs(z)))
        log_keep = -sp
        if diagonal:
            log_keep = jnp.where(strictly_before, log_keep, 0.0)
        hi = log_keep.astype(BF16)
        lo = (log_keep - hi.astype(F32)).astype(BF16)
        suffix = (jnp.dot(hi, later, preferred_element_type=F32)
                  + jnp.dot(lo, later, preferred_element_type=F32))
        w = jnp.exp((z - sp) + suffix + run)
        if diagonal:
            w = jnp.where(strictly_before, w, 0.0)
        acc = acc + jnp.dot(w.astype(BF16), v, preferred_element_type=F32)
        run = run + suffix[:, :1] + log_keep[:, :1]
        return run, acc

    run0 = jnp.zeros((tile, 1), F32)
    acc0 = jnp.zeros((tile, HEAD_DIM), F32)
    run, acc = key_tile(qi, True, run0, acc0)

    def cond(carry):
        j, alive, _, _ = carry
        return jnp.logical_and(j >= 0, alive > 0)

    def body(carry):
        j, _, run, acc = carry
        run, acc = key_tile(j, False, run, acc)
        alive = (jnp.max(run) > -SB_EXIT).astype(jnp.int32)
        return j - 1, alive, run, acc

    alive0 = (jnp.max(run) > -SB_EXIT).astype(jnp.int32)
    _, _, _, acc = lax.while_loop(cond, body, (qi - 1, alive0, run, acc))
    o_ref[0] = acc.astype(o_ref.dtype)


def sb_attention(qkv, tile=256):
    b, t, _ = qkv.shape
    tile = min(tile, t)
    assert t % tile == 0
    return pl.pallas_call(
        functools.partial(_sb_kernel, tile=tile),
        grid=(b, SB_HEADS, t // tile),
        in_specs=[
            pl.BlockSpec((1, tile, HEAD_DIM), lambda bi, h, i: (bi, i, h)),
            pl.BlockSpec((1, t, HEAD_DIM), lambda bi, h, i: (bi, 0, SB_HEADS + h)),
            pl.BlockSpec((1, t, HEAD_DIM), lambda bi, h, i: (bi, 0, 2 * SB_HEADS + h)),
        ],
        out_specs=pl.BlockSpec((1, tile, HEAD_DIM), lambda bi, h, i: (bi, i, h)),
        out_shape=jax.ShapeDtypeStruct((b, t, SB_WIDTH), BF16),
        compiler_params=_params(("parallel", "parallel", "arbitrary")),
        name="sb_attention",
    )(qkv, qkv, qkv)


def _ssd_kernel(zxd_ref, halo_ref, cw_ref, cb_ref, dtb_ref, alog_ref, dskip_ref, nw_ref, o_ref, state_ref):
    c = pl.program_id(1)
    L, N, P = SSM_CHUNK, SSM_STATE, SSM_HEAD_DIM
    n_pairs = SSM_HEADS // 2
    pairs_per_group = n_pairs // SSM_GROUPS

    @pl.when(c == 0)
    def _():
        state_ref[...] = jnp.zeros_like(state_ref)

    blk = zxd_ref[0]
    z = blk[:, :SSM_INNER]
    xbc = blk[:, SSM_INNER:SSM_INNER + CONV_DIM]
    dt_raw = blk[:, SSM_INNER + CONV_DIM:]
    halo = jnp.where(c == 0, 0.0, halo_ref[0][:, SSM_INNER:SSM_INNER + CONV_DIM])

    cw = cw_ref[...]
    conv = cw[3:4] * xbc + cb_ref[...]
    for s in range(1, SSM_CONV):
        conv = conv + cw[3 - s:4 - s] * _shift_rows(xbc, halo, s)
    xbc = conv * _sigmoid(conv)
    xs = xbc[:, :SSM_INNER]
    b_in = xbc[:, SSM_INNER:SSM_INNER + SSM_GROUPS * N]
    c_in = xbc[:, SSM_INNER + SSM_GROUPS * N:]

    pre = dt_raw + dtb_ref[...]
    dt = jnp.maximum(pre, 0.0) + jnp.log1p(jnp.exp(-jnp.abs(pre)))
    a = -jnp.exp(alog_ref[...])
    row = lax.broadcasted_iota(jnp.int32, (L, L), 0)
    col = lax.broadcasted_iota(jnp.int32, (L, L), 1)
    causal = row >= col
    tri = causal.astype(F32)
    a_cs = jnp.dot(tri, dt * a, preferred_element_type=F32, precision=lax.Precision.HIGHEST)
    a_cs_t = a_cs.T
    a_end = a_cs[L - 1:L, :]

    lane = lax.broadcasted_iota(jnp.int32, (L, LANES), 1)
    first_half = lane < P
    srow = lax.broadcasted_iota(jnp.int32, (2 * N, LANES), 0)
    slane = lax.broadcasted_iota(jnp.int32, (2 * N, LANES), 1)
    block_diag = (srow < N) == (slane < P)

    scores = []
    for g in range(SSM_GROUPS):
        cg = c_in[:, g * N:(g + 1) * N].astype(BF16)
        bg = b_in[:, g * N:(g + 1) * N].astype(BF16)
        scores.append(lax.dot_general(cg, bg, (((1,), (1,)), ((), ())), preferred_element_type=F32))

    ys = []
    for j in range(n_pairs):
        g = j // pairs_per_group
        h0, h1 = 2 * j, 2 * j + 1
        cg = c_in[:, g * N:(g + 1) * N]
        bg = b_in[:, g * N:(g + 1) * N]
        xs_pair = xs[:, j * LANES:(j + 1) * LANES]
        dt_pair = jnp.where(first_half, dt[:, h0:h0 + 1], dt[:, h1:h1 + 1])
        xdt = xs_pair * dt_pair
        sd, c_dec, b_dec, cd = [], [], [], []
        for h in (h0, h1):
            col_h = a_cs[:, h:h + 1]
            row_h = a_cs_t[h:h + 1, :]
            decay = jnp.where(causal, jnp.exp(col_h - row_h), 0.0)
            sd.append((scores[g] * decay).astype(BF16))
            c_dec.append((cg * jnp.exp(col_h)).astype(BF16))
            b_dec.append(bg * jnp.exp(a_end[:, h:h + 1] - col_h))
            cd.append(jnp.exp(a_end[:, h:h + 1]))
        xdt_bf = xdt.astype(BF16)
        zero = jnp.zeros_like(xdt_bf)
        rhs = jnp.concatenate([jnp.where(first_half, xdt_bf, zero), jnp.where(first_half, zero, xdt_bf)], axis=0)
        y = jnp.dot(jnp.concatenate(sd, axis=1), rhs, preferred_element_type=F32)
        state = state_ref[j]
        y = y + jnp.dot(jnp.concatenate(c_dec, axis=1), state.astype(BF16), preferred_element_type=F32)
        b_pair_t = jnp.concatenate(b_dec, axis=1).T.astype(BF16)
        upd = jnp.dot(b_pair_t, xdt_bf, preferred_element_type=F32)
        cd_rows = jnp.where(srow < N, cd[0], cd[1])
        state_ref[j] = jnp.where(block_diag, state * cd_rows + upd, 0.0)
        ys.append(y + dskip_ref[:, j * LANES:(j + 1) * LANES] * xs_pair)

    y = jnp.concatenate(ys, axis=1) * (z * _sigmoid(z))
    o_ref[0] = _rms(y, nw_ref[...]).astype(o_ref.dtype)


def ssd(zxd, cw, cb, dt_bias, a_log, d_skip, norm_w):
    b, t, width = zxd.shape
    L = SSM_CHUNK
    assert t % L == 0 and width == ZXD_WIDTH
    pad = lambda p: jnp.pad(p.astype(F32), (0, DT_PAD - SSM_HEADS)).reshape(1, DT_PAD)
    dskip_lanes = jnp.repeat(d_skip.astype(F32), SSM_HEAD_DIM).reshape(1, SSM_INNER)
    full = lambda shape: pl.BlockSpec(shape, lambda bi, c: (0,) * len(shape))
    return pl.pallas_call(
        _ssd_kernel,
        grid=(b, t // L),
        in_specs=[
            pl.BlockSpec((1, L, width), lambda bi, c: (bi, c, 0)),
            pl.BlockSpec((1, SUBLANES, width), lambda bi, c: (bi, jnp.maximum(c * (L // SUBLANES) - 1, 0), 0)),
            full((SSM_CONV, CONV_DIM)), full((1, CONV_DIM)), full((1, DT_PAD)), full((1, DT_PAD)),
            full((1, SSM_INNER)), full((1, SSM_INNER)),
        ],
        out_specs=pl.BlockSpec((1, L, SSM_INNER), lambda bi, c: (bi, c, 0)),
        out_shape=jax.ShapeDtypeStruct((b, t, SSM_INNER), BF16),
        scratch_shapes=[pltpu.VMEM((SSM_HEADS // 2, 2 * SSM_STATE, LANES), F32)],
        compiler_params=_params(("parallel", "arbitrary")),
        name="ssd",
    )(zxd, zxd, cw, cb.reshape(1, CONV_DIM), pad(dt_bias), pad(a_log), dskip_lanes, norm_w.reshape(1, SSM_INNER))


def _dilated_kernel(q_ref, kp_ref, kc_ref, vp_ref, vc_ref, o_ref, qf, kf, vf, of, lf, *, span):
    si = pl.program_id(2)
    blk = DIL_BLOCK
    qf[...] = q_ref[0].astype(F32)
    kf[:span] = kp_ref[0].astype(F32)
    kf[span:] = kc_ref[0].astype(F32)
    vf[:span] = vp_ref[0].astype(F32)
    vf[span:] = vc_ref[0].astype(F32)

    qpos = lax.broadcasted_iota(jnp.int32, (blk, 2 * blk), 0)
    kpos = lax.broadcasted_iota(jnp.int32, (blk, 2 * blk), 1)
    dist = qpos + blk - kpos
    units = span // blk

    for bi, (window, r) in enumerate(DIL_PATTERNS):
        reach = window // r
        band = jnp.logical_and(dist >= 0, dist <= reach)
        blocks_per_class = units // r

        def unit(u, carry, bi=bi, r=r, band=band, blocks_per_class=blocks_per_class):
            cls = u // blocks_per_class
            nb = u % blocks_per_class
            q0 = cls + r * blk * nb
            k0 = span + q0 - r * blk
            if r == 1:
                q = qf[pl.ds(q0, blk), :]
                k = kf[pl.ds(k0, 2 * blk), :]
                v = vf[pl.ds(k0, 2 * blk), :]
            else:
                q = qf[pl.ds(q0, blk, stride=r), :]
                k = kf[pl.ds(k0, 2 * blk, stride=r), :]
                v = vf[pl.ds(k0, 2 * blk, stride=r), :]
            s = lax.dot_general(q.astype(BF16), k.astype(BF16), (((1,), (1,)), ((), ())),
                                preferred_element_type=F32)
            has_prev = jnp.logical_or(si > 0, nb > 0)
            valid = jnp.logical_and(band, jnp.logical_or(has_prev, kpos >= blk))
            s = jnp.where(valid, s, NEG_BIG)
            m = jnp.max(s, axis=-1, keepdims=True)
            p = jnp.exp(s - m)
            den = jnp.sum(p, axis=-1, keepdims=True)
            o = jnp.dot(p.astype(BF16), v.astype(BF16), preferred_element_type=F32) / den
            lse = jnp.broadcast_to(m + jnp.log(den), (blk, HEAD_DIM))
            if r == 1:
                of[bi, pl.ds(q0, blk), :] = o
                lf[bi, pl.ds(q0, blk), :] = lse
            else:
                of[bi, pl.ds(q0, blk, stride=r), :] = o
                lf[bi, pl.ds(q0, blk, stride=r), :] = lse
            return carry

        lax.fori_loop(0, units, unit, 0)

    l0, l1, l2 = lf[0], lf[1], lf[2]
    m = jnp.maximum(jnp.maximum(l0, l1), l2)
    e0, e1, e2 = jnp.exp(l0 - m), jnp.exp(l1 - m), jnp.exp(l2 - m)
    out = (e0 * of[0] + e1 * of[1] + e2 * of[2]) / (e0 + e1 + e2)
    o_ref[0] = out.astype(o_ref.dtype)


def dilated_attention(qkv):
    b, t, _ = qkv.shape
    span = DIL_BLOCK * max(r for _, r in DIL_PATTERNS)
    assert t % span == 0
    nh = DIL_HEADS
    cur = lambda off: pl.BlockSpec((1, span, HEAD_DIM), lambda bi, h, s: (bi, s, off + h))
    prev = lambda off: pl.BlockSpec((1, span, HEAD_DIM), lambda bi, h, s: (bi, jnp.maximum(s - 1, 0), off + h))
    return pl.pallas_call(
        functools.partial(_dilated_kernel, span=span),
        grid=(b, nh, t // span),
        in_specs=[cur(0), prev(nh), cur(nh), prev(2 * nh), cur(2 * nh)],
        out_specs=pl.BlockSpec((1, span, HEAD_DIM), lambda bi, h, s: (bi, s, h)),
        out_shape=jax.ShapeDtypeStruct((b, t, nh * HEAD_DIM), BF16),
        scratch_shapes=[
            pltpu.VMEM((span, HEAD_DIM), F32),
            pltpu.VMEM((2 * span, HEAD_DIM), F32),
            pltpu.VMEM((2 * span, HEAD_DIM), F32),
            pltpu.VMEM((len(DIL_PATTERNS), span, HEAD_DIM), F32),
            pltpu.VMEM((len(DIL_PATTERNS), span, HEAD_DIM), F32),
        ],
        compiler_params=_params(("parallel", "parallel", "arbitrary")),
        name="dilated_attention",
    )(qkv, qkv, qkv, qkv, qkv)


def _rmsnorm_kernel(x_ref, w_ref, o_ref):
    o_ref[...] = _rms(x_ref[...], w_ref[...])


def rmsnorm(x, w, tm=512):
    m, d = x.shape
    tm = min(tm, m)
    assert m % tm == 0
    return pl.pallas_call(
        _rmsnorm_kernel,
        grid=(m // tm,),
        in_specs=[pl.BlockSpec((tm, d), lambda i: (i, 0)), pl.BlockSpec((1, d), lambda i: (0, 0))],
        out_specs=pl.BlockSpec((tm, d), lambda i: (i, 0)),
        out_shape=jax.ShapeDtypeStruct((m, d), F32),
        compiler_params=_params(("parallel",)),
        name="rmsnorm",
    )(x, w.reshape(1, d))


def _even_mixer(h, bsz, seq, nw, w_in, conv_w, conv_b, dt_bias, a_log, d_skip, ssm_norm_w, w_out):
    scale = HEAD_DIM ** -0.5
    qkv_cols = 3 * SB_WIDTH
    w_qkv = jnp.concatenate([w_in[:, :SB_WIDTH] * scale, w_in[:, SB_WIDTH:qkv_cols]], axis=1).astype(BF16)
    w_zxd = jnp.pad(w_in[:, qkv_cols:], ((0, 0), (0, DT_PAD - SSM_HEADS))).astype(BF16)
    qkv = norm_matmul(h, nw, w_qkv, BF16)
    zxd = norm_matmul(h, nw, w_zxd, F32, tn=ZXD_WIDTH // 3)
    o_a = sb_attention(qkv.reshape(bsz, seq, qkv_cols)).reshape(bsz * seq, SB_WIDTH)
    o_b = ssd(zxd.reshape(bsz, seq, ZXD_WIDTH), conv_w, conv_b, dt_bias, a_log, d_skip,
              ssm_norm_w).reshape(bsz * seq, SSM_INNER)
    w_out = w_out.astype(BF16)
    return matmul_residual([o_a, o_b], [w_out[:SB_WIDTH], w_out[SB_WIDTH:]], h)


def _odd_mixer(h, bsz, seq, nw, w_in, w_out):
    scale = HEAD_DIM ** -0.5
    w_qkv = jnp.concatenate([w_in[:, :D_MODEL] * scale, w_in[:, D_MODEL:]], axis=1).astype(BF16)
    qkv = norm_matmul(h, nw, w_qkv, BF16)
    o = dilated_attention(qkv.reshape(bsz, seq, 3 * D_MODEL)).reshape(bsz * seq, D_MODEL)
    return matmul_residual([o], [w_out.astype(BF16)], h)


def kernel(x, mix_norm_w, ffn_norm_w, final_norm_w, ev_w_in, ev_conv_w, ev_conv_b, ev_dt_bias, ev_a_log, ev_d_skip, ev_ssm_norm_w, ev_w_out, od_w_in, od_w_out, ffn_w_gate, ffn_w_up, ffn_conv_w, ffn_conv_b, ffn_w_down):
    bsz, seq, d = x.shape
    h = x.reshape(bsz * seq, d)
    for layer in range(DEPTH):
        i = layer // 2
        if layer % 2 == 0:
            h = _even_mixer(h, bsz, seq, mix_norm_w[layer], ev_w_in[i], ev_conv_w[i], ev_conv_b[i],
                            ev_dt_bias[i], ev_a_log[i], ev_d_skip[i], ev_ssm_norm_w[i], ev_w_out[i])
        else:
            h = _odd_mixer(h, bsz, seq, mix_norm_w[layer], od_w_in[i], od_w_out[i])
        h = conv_ffn(h, seq, ffn_norm_w[layer], ffn_w_gate[layer].astype(BF16), ffn_w_up[layer].astype(BF16),
                     ffn_conv_w[layer], ffn_conv_b[layer], ffn_w_down[layer].astype(BF16))
    return rmsnorm(h, final_norm_w).reshape(bsz, seq, d)
```

```python
import functools

import jax
import jax.numpy as jnp
from jax import lax
from jax.experimental import pallas as pl
from jax.experimental.pallas import tpu as pltpu

F32 = jnp.float32
BF16 = jnp.bfloat16

D_MODEL = 2048
DEPTH = 4
SB_HEADS = 8
HEAD_DIM = 128
SB_WIDTH = SB_HEADS * HEAD_DIM
SSM_HEAD_DIM = 64
SSM_INNER = 1024
SSM_HEADS = 16
SSM_GROUPS = 2
SSM_STATE = 128
SSM_CONV = 4
SSM_CHUNK = 128
CONV_DIM = SSM_INNER + 2 * SSM_GROUPS * SSM_STATE
DIL_HEADS = 16
DIL_PATTERNS = ((128, 1), (512, 4), (2048, 16))
DIL_BLOCK = 128
D_FF = 5632
FFN_CONV = 3
EPS = 1e-6

LANES = 128
SUBLANES = 8
VMEM_LIMIT = 56 * 1024 * 1024
DT_PAD = LANES
ZXD_WIDTH = SSM_INNER + CONV_DIM + DT_PAD

SB_EXIT_LOG2 = 160.0
LOG2_E = 1.4426950408889634
NEG_BIG = -1e30


def _params(sem, vmem=VMEM_LIMIT):
    return pltpu.CompilerParams(dimension_semantics=sem, vmem_limit_bytes=vmem)


def _rms(x, w):
    ms = jnp.mean(x * x, axis=-1, keepdims=True)
    return x * lax.rsqrt(ms + EPS) * w


def _sigmoid(x):
    return 1.0 / (1.0 + jnp.exp(-x))


def _shift_rows(x, halo, s):
    n = x.shape[0]
    r = pltpu.roll(x, s, axis=0)
    hr = pltpu.roll(halo, s, axis=0)
    rid = lax.broadcasted_iota(jnp.int32, hr.shape, 0)
    top = jnp.where(rid < s, hr, r[:SUBLANES])
    return jnp.concatenate([top, r[SUBLANES:]], axis=0) if n > SUBLANES else top


def _norm_matmul_kernel(x_ref, nw_ref, w_ref, o_ref, xn_ref):
    @pl.when(pl.program_id(1) == 0)
    def _():
        xn_ref[...] = _rms(x_ref[...], nw_ref[...]).astype(BF16)

    o_ref[...] = jnp.dot(xn_ref[...], w_ref[...], preferred_element_type=F32).astype(o_ref.dtype)


def norm_matmul(x, nw, w, out_dtype, tm=1024, tn=1536):
    m, d = x.shape
    n = w.shape[1]
    tm, tn = min(tm, m), min(tn, n)
    assert m % tm == 0 and n % tn == 0
    return pl.pallas_call(
        _norm_matmul_kernel,
        grid=(m // tm, n // tn),
        in_specs=[
            pl.BlockSpec((tm, d), lambda i, j: (i, 0)),
            pl.BlockSpec((1, d), lambda i, j: (0, 0)),
            pl.BlockSpec((d, tn), lambda i, j: (0, j)),
        ],
        out_specs=pl.BlockSpec((tm, tn), lambda i, j: (i, j)),
        out_shape=jax.ShapeDtypeStruct((m, n), out_dtype),
        scratch_shapes=[pltpu.VMEM((tm, d), BF16)],
        compiler_params=_params(("parallel", "arbitrary")),
        name="norm_matmul",
    )(x, nw.reshape(1, d), w)


def _matmul_residual_kernel(*refs, n_in):
    xs, ws = refs[:n_in], refs[n_in:2 * n_in]
    h_ref, nw_ref, o_ref, xn_ref = refs[2 * n_in:]
    acc = h_ref[...]
    for x_ref, w_ref in zip(xs, ws):
        acc = acc + jnp.dot(x_ref[...], w_ref[...], preferred_element_type=F32)
    o_ref[...] = acc
    xn_ref[...] = _rms(acc, nw_ref[...]).astype(BF16)


def matmul_residual(xs, ws, h, nw, tm=512):
    m, n = h.shape
    tm = min(tm, m)
    assert m % tm == 0
    in_specs = [pl.BlockSpec((tm, x.shape[1]), lambda i: (i, 0)) for x in xs]
    in_specs += [pl.BlockSpec(w.shape, lambda i: (0, 0)) for w in ws]
    in_specs += [pl.BlockSpec((tm, n), lambda i: (i, 0)), pl.BlockSpec((1, n), lambda i: (0, 0))]
    return pl.pallas_call(
        functools.partial(_matmul_residual_kernel, n_in=len(xs)),
        grid=(m // tm,),
        in_specs=in_specs,
        out_specs=[pl.BlockSpec((tm, n), lambda i: (i, 0)), pl.BlockSpec((tm, n), lambda i: (i, 0))],
        out_shape=[jax.ShapeDtypeStruct((m, n), F32), jax.ShapeDtypeStruct((m, n), BF16)],
        compiler_params=_params(("parallel",)),
        name="matmul_residual",
    )(*xs, *ws, h, nw.reshape(1, n))


def _conv_ffn_kernel(xn_ref, h_ref, wg_ref, wu_ref, cw_ref, cb_ref, wd_ref, o_ref, act_ref, carry_ref,
                     *, n_f, tf, tiles_per_seq):
    i, j = pl.program_id(0), pl.program_id(1)

    @pl.when(j < n_f)
    def _():
        @pl.when(i % tiles_per_seq == 0)
        def _():
            carry_ref[j] = jnp.zeros(carry_ref.shape[1:], F32)

        xn = xn_ref[...]
        g = jnp.dot(xn, wg_ref[...], preferred_element_type=F32)
        u = jnp.dot(xn, wu_ref[...], preferred_element_type=F32)
        halo = carry_ref[j]
        carry_ref[j] = g[-SUBLANES:]
        cw = cw_ref[...]
        c = cw[2:3] * g + cw[1:2] * _shift_rows(g, halo, 1) + cw[0:1] * _shift_rows(g, halo, 2) + cb_ref[...]
        act_ref[j] = (c * _sigmoid(c) * u).astype(BF16)

    @pl.when(j >= n_f)
    def _():
        acc = h_ref[...]
        for f in range(n_f):
            acc = acc + jnp.dot(act_ref[f], wd_ref[f * tf:(f + 1) * tf, :], preferred_element_type=F32)
        o_ref[...] = acc


def conv_ffn(xn, h, seq_len, wg, wu, cw, cb, wd, tm=1024, tf=512, tn=512):
    m, d = h.shape
    dff = wg.shape[1]
    tm = min(tm, seq_len)
    assert seq_len % tm == 0 and dff % tf == 0 and m % seq_len == 0 and d % tn == 0
    n_f = dff // tf
    gate = lambda i, j: (0, jnp.minimum(j, n_f - 1))
    down = lambda i, j: (0, jnp.maximum(j - n_f, 0))
    out = lambda i, j: (i, jnp.maximum(j - n_f, 0))
    return pl.pallas_call(
        functools.partial(_conv_ffn_kernel, n_f=n_f, tf=tf, tiles_per_seq=seq_len // tm),
        grid=(m // tm, n_f + d // tn),
        in_specs=[
            pl.BlockSpec((tm, d), lambda i, j: (i, 0)),
            pl.BlockSpec((tm, tn), out),
            pl.BlockSpec((d, tf), gate),
            pl.BlockSpec((d, tf), gate),
            pl.BlockSpec((FFN_CONV, tf), gate),
            pl.BlockSpec((1, tf), gate),
            pl.BlockSpec((dff, tn), down),
        ],
        out_specs=pl.BlockSpec((tm, tn), out),
        out_shape=jax.ShapeDtypeStruct((m, d), F32),
        scratch_shapes=[pltpu.VMEM((n_f, tm, tf), BF16), pltpu.VMEM((n_f, SUBLANES, tf), F32)],
        compiler_params=_params(("arbitrary", "arbitrary")),
        name="conv_ffn",
    )(xn, h, wg, wu, cw, cb.reshape(1, dff), wd)


def _sb_kernel(q_ref, k_ref, v_ref, o_ref, *, tile, heads):
    qi = pl.program_id(2)
    row = lax.broadcasted_iota(jnp.int32, (tile, tile), 0)
    col = lax.broadcasted_iota(jnp.int32, (tile, tile), 1)
    later = (row > col).astype(BF16)
    strictly_before = col < row

    def logits(q, hs, j, diagonal):
        k = k_ref[0, pl.ds(pl.multiple_of(j * tile, tile), tile), hs]
        z = lax.dot_general(q, k, (((1,), (1,)), ((), ())), preferred_element_type=F32)
        sp = jnp.maximum(z, 0.0) + jnp.log2(1.0 + jnp.exp2(-jnp.abs(z)))
        log_keep = -sp
        if diagonal:
            log_keep = jnp.where(strictly_before, log_keep, 0.0)
        hi = log_keep.astype(BF16)
        lo = (log_keep - hi.astype(F32)).astype(BF16)
        suffix = (jnp.dot(hi, later, preferred_element_type=F32)
                  + jnp.dot(lo, later, preferred_element_type=F32))
        return (z - sp) + suffix, suffix[:, :1] + log_keep[:, :1]

    def weighted(hs, j, w):
        v = v_ref[0, pl.ds(pl.multiple_of(j * tile, tile), tile), hs]
        return jnp.dot(w.astype(BF16), v, preferred_element_type=F32)

    jb = jnp.maximum(qi - 1, 0)
    no_left_tile = jnp.where(qi > 0, 0.0, NEG_BIG)
    head_slices = [slice(hd * HEAD_DIM, (hd + 1) * HEAD_DIM) for hd in range(heads)]
    first_two = []
    for hs in head_slices:
        q = q_ref[0, :, hs]
        la, ta = logits(q, hs, qi, True)
        lb, tb = logits(q, hs, jb, False)
        acc = weighted(hs, qi, jnp.where(strictly_before, jnp.exp2(la), 0.0))
        acc = acc + weighted(hs, jb, jnp.exp2(lb + (ta + no_left_tile)))
        first_two.append((q, ta + tb, acc))

    for hs, (q, run, acc) in zip(head_slices, first_two):

        def cond(carry):
            j, alive, _, _ = carry
            return jnp.logical_and(j >= 0, alive > 0)

        def body(carry, q=q, hs=hs):
            j, _, run, acc = carry
            lj, tj = logits(q, hs, j, False)
            acc = acc + weighted(hs, j, jnp.exp2(lj + run))
            run = run + tj
            alive = (jnp.max(run) > -SB_EXIT_LOG2).astype(jnp.int32)
            return j - 1, alive, run, acc

        alive0 = (jnp.max(run) > -SB_EXIT_LOG2).astype(jnp.int32)
        _, _, _, acc = lax.while_loop(cond, body, (qi - 2, alive0, run, acc))
        o_ref[0, :, hs] = acc.astype(o_ref.dtype)


def sb_attention(qkv, tile=256, heads=2):
    b, t, _ = qkv.shape
    tile = min(tile, t)
    assert t % tile == 0 and SB_HEADS % heads == 0
    groups = SB_HEADS // heads
    width = heads * HEAD_DIM
    return pl.pallas_call(
        functools.partial(_sb_kernel, tile=tile, heads=heads),
        grid=(b, groups, t // tile),
        in_specs=[
            pl.BlockSpec((1, tile, width), lambda bi, g, i: (bi, i, g)),
            pl.BlockSpec((1, t, width), lambda bi, g, i: (bi, 0, groups + g)),
            pl.BlockSpec((1, t, width), lambda bi, g, i: (bi, 0, 2 * groups + g)),
        ],
        out_specs=pl.BlockSpec((1, tile, width), lambda bi, g, i: (bi, i, g)),
        out_shape=jax.ShapeDtypeStruct((b, t, SB_WIDTH), BF16),
        compiler_params=_params(("parallel", "parallel", "arbitrary")),
        name="sb_attention",
    )(qkv, qkv, qkv)


def _ssd_kernel(zxd_ref, halo_ref, cw_ref, cb_ref, dtb_ref, alog_ref, dskip_ref, nw_ref, o_ref, state_ref):
    c = pl.program_id(1)
    L, N, P = SSM_CHUNK, SSM_STATE, SSM_HEAD_DIM
    n_pairs = SSM_HEADS // 2
    pairs_per_group = n_pairs // SSM_GROUPS

    @pl.when(c == 0)
    def _():
        state_ref[...] = jnp.zeros_like(state_ref)

    blk = zxd_ref[0]
    z = blk[:, :SSM_INNER]
    xbc = blk[:, SSM_INNER:SSM_INNER + CONV_DIM]
    dt_raw = blk[:, SSM_INNER + CONV_DIM:]
    halo = jnp.where(c == 0, 0.0, halo_ref[0][:, SSM_INNER:SSM_INNER + CONV_DIM])

    cw = cw_ref[...]
    conv = cw[3:4] * xbc + cb_ref[...]
    for s in range(1, SSM_CONV):
        conv = conv + cw[3 - s:4 - s] * _shift_rows(xbc, halo, s)
    xbc = conv * _sigmoid(conv)
    xs = xbc[:, :SSM_INNER]
    b_in = xbc[:, SSM_INNER:SSM_INNER + SSM_GROUPS * N]
    c_in = xbc[:, SSM_INNER + SSM_GROUPS * N:]

    pre = dt_raw + dtb_ref[...]
    dt = jnp.maximum(pre, 0.0) + jnp.log1p(jnp.exp(-jnp.abs(pre)))
    a = -jnp.exp(alog_ref[...])
    row = lax.broadcasted_iota(jnp.int32, (L, L), 0)
    col = lax.broadcasted_iota(jnp.int32, (L, L), 1)
    causal = row >= col
    tri = causal.astype(F32)
    a_cs = jnp.dot(tri, dt * a, preferred_element_type=F32, precision=lax.Precision.HIGHEST)
    a_cs_t = a_cs.T
    a_end = a_cs[L - 1:L, :]

    lane = lax.broadcasted_iota(jnp.int32, (L, LANES), 1)
    first_half = lane < P
    srow = lax.broadcasted_iota(jnp.int32, (2 * N, LANES), 0)
    slane = lax.broadcasted_iota(jnp.int32, (2 * N, LANES), 1)
    block_diag = (srow < N) == (slane < P)

    scores = []
    for g in range(SSM_GROUPS):
        cg = c_in[:, g * N:(g + 1) * N].astype(BF16)
        bg = b_in[:, g * N:(g + 1) * N].astype(BF16)
        scores.append(lax.dot_general(cg, bg, (((1,), (1,)), ((), ())), preferred_element_type=F32))

    ys = []
    for j in range(n_pairs):
        g = j // pairs_per_group
        h0, h1 = 2 * j, 2 * j + 1
        cg = c_in[:, g * N:(g + 1) * N]
        bg = b_in[:, g * N:(g + 1) * N]
        xs_pair = xs[:, j * LANES:(j + 1) * LANES]
        dt_pair = jnp.where(first_half, dt[:, h0:h0 + 1], dt[:, h1:h1 + 1])
        xdt = xs_pair * dt_pair
        sd, c_dec, b_dec, cd = [], [], [], []
        for h in (h0, h1):
            col_h = a_cs[:, h:h + 1]
            row_h = a_cs_t[h:h + 1, :]
            decay = jnp.where(causal, jnp.exp(col_h - row_h), 0.0)
            sd.append((scores[g] * decay).astype(BF16))
            c_dec.append((cg * jnp.exp(col_h)).astype(BF16))
            b_dec.append(bg * jnp.exp(a_end[:, h:h + 1] - col_h))
            cd.append(jnp.exp(a_end[:, h:h + 1]))
        xdt_bf = xdt.astype(BF16)
        zero = jnp.zeros_like(xdt_bf)
        rhs = jnp.concatenate([jnp.where(first_half, xdt_bf, zero), jnp.where(first_half, zero, xdt_bf)], axis=0)
        y = jnp.dot(jnp.concatenate(sd, axis=1), rhs, preferred_element_type=F32)
        state = state_ref[j]
        y = y + jnp.dot(jnp.concatenate(c_dec, axis=1), state.astype(BF16), preferred_element_type=F32)
        b_pair_t = jnp.concatenate(b_dec, axis=1).T.astype(BF16)
        upd = jnp.dot(b_pair_t, xdt_bf, preferred_element_type=F32)
        cd_rows = jnp.where(srow < N, cd[0], cd[1])
        state_ref[j] = jnp.where(block_diag, state * cd_rows + upd, 0.0)
        ys.append(y + dskip_ref[:, j * LANES:(j + 1) * LANES] * xs_pair)

    y = jnp.concatenate(ys, axis=1) * (z * _sigmoid(z))
    o_ref[0] = _rms(y, nw_ref[...]).astype(o_ref.dtype)


def ssd(zxd, cw, cb, dt_bias, a_log, d_skip, norm_w):
    b, t, width = zxd.shape
    L = SSM_CHUNK
    assert t % L == 0 and width == ZXD_WIDTH
    pad = lambda p: jnp.pad(p.astype(F32), (0, DT_PAD - SSM_HEADS)).reshape(1, DT_PAD)
    dskip_lanes = jnp.repeat(d_skip.astype(F32), SSM_HEAD_DIM).reshape(1, SSM_INNER)
    full = lambda shape: pl.BlockSpec(shape, lambda bi, c: (0,) * len(shape))
    return pl.pallas_call(
        _ssd_kernel,
        grid=(b, t // L),
        in_specs=[
            pl.BlockSpec((1, L, width), lambda bi, c: (bi, c, 0)),
            pl.BlockSpec((1, SUBLANES, width), lambda bi, c: (bi, jnp.maximum(c * (L // SUBLANES) - 1, 0), 0)),
            full((SSM_CONV, CONV_DIM)), full((1, CONV_DIM)), full((1, DT_PAD)), full((1, DT_PAD)),
            full((1, SSM_INNER)), full((1, SSM_INNER)),
        ],
        out_specs=pl.BlockSpec((1, L, SSM_INNER), lambda bi, c: (bi, c, 0)),
        out_shape=jax.ShapeDtypeStruct((b, t, SSM_INNER), BF16),
        scratch_shapes=[pltpu.VMEM((SSM_HEADS // 2, 2 * SSM_STATE, LANES), F32)],
        compiler_params=_params(("parallel", "arbitrary")),
        name="ssd",
    )(zxd, zxd, cw, cb.reshape(1, CONV_DIM), pad(dt_bias), pad(a_log), dskip_lanes, norm_w.reshape(1, SSM_INNER))


def _dilated_kernel(q_ref, k_ref, v_ref, o_ref, xf, qd, kd, vd, bias, of, lf, *, span):
    si = pl.program_id(2)
    blk = DIL_BLOCK
    units = span // blk
    cur = si % 2
    prv = 1 - cur

    @pl.when(si == 0)
    def _():
        kd[:, 1] = jnp.zeros(kd.shape[:1] + kd.shape[2:], kd.dtype)
        vd[:, 1] = jnp.zeros(vd.shape[:1] + vd.shape[2:], vd.dtype)
        qpos = lax.broadcasted_iota(jnp.int32, (blk, 2 * blk), 0)
        kpos = lax.broadcasted_iota(jnp.int32, (blk, 2 * blk), 1)
        dist = qpos + blk - kpos
        for bi, (window, r) in enumerate(DIL_PATTERNS):
            band = jnp.logical_and(dist >= 0, dist <= window // r)
            bias[bi, 0] = jnp.where(band, 0.0, NEG_BIG)
            bias[bi, 1] = jnp.where(jnp.logical_and(band, kpos >= blk), 0.0, NEG_BIG)

    xf[0] = q_ref[0].astype(F32)
    xf[1] = k_ref[0].astype(F32)
    xf[2] = v_ref[0].astype(F32)
    for bi, (_, r) in enumerate(DIL_PATTERNS):
        rows = span // r
        if r == 1:
            kd[bi, cur] = k_ref[0]
            vd[bi, cur] = v_ref[0]
            continue
        for c in range(r):
            src = pl.ds(c, rows, stride=r)
            dst = pl.ds(c * rows, rows)
            qd[bi, dst, :] = xf[0, src, :].astype(BF16)
            kd[bi, cur, dst, :] = xf[1, src, :].astype(BF16)
            vd[bi, cur, dst, :] = xf[2, src, :].astype(BF16)

    first = jnp.where(si == 0, 1, 0)
    for bi, (_, r) in enumerate(DIL_PATTERNS):
        rows = span // r
        for u in range(units):
            cls, nb = divmod(u, units // r)
            base = cls * rows + nb * blk
            here = pl.ds(base, blk)
            q = q_ref[0, here, :] if r == 1 else qd[bi, here, :]
            if nb > 0:
                before = pl.ds(base - blk, blk)
                k_prev, v_prev = kd[bi, cur, before, :], vd[bi, cur, before, :]
                b = bias[bi, 0]
            else:
                before = pl.ds(cls * rows + rows - blk, blk)
                k_prev, v_prev = kd[bi, prv, before, :], vd[bi, prv, before, :]
                b = bias[bi, first]
            k = jnp.concatenate([k_prev, kd[bi, cur, here, :]], axis=0)
            v = jnp.concatenate([v_prev, vd[bi, cur, here, :]], axis=0)
            s = lax.dot_general(q, k, (((1,), (1,)), ((), ())), preferred_element_type=F32) + b
            m = jnp.max(s, axis=-1, keepdims=True)
            p = jnp.exp(s - m)
            den = jnp.sum(p, axis=-1, keepdims=True)
            o = jnp.dot(p.astype(BF16), v, preferred_element_type=F32) / den
            lse = jnp.broadcast_to(m + jnp.log(den), (blk, HEAD_DIM))
            natural = pl.ds(base, blk) if r == 1 else pl.ds(cls + r * blk * nb, blk, stride=r)
            of[bi, natural, :] = o
            lf[bi, natural, :] = lse

    l0, l1, l2 = lf[0], lf[1], lf[2]
    m = jnp.maximum(jnp.maximum(l0, l1), l2)
    e0, e1, e2 = jnp.exp(l0 - m), jnp.exp(l1 - m), jnp.exp(l2 - m)
    out = (e0 * of[0] + e1 * of[1] + e2 * of[2]) / (e0 + e1 + e2)
    o_ref[0] = out.astype(o_ref.dtype)


def dilated_attention(qkv):
    b, t, _ = qkv.shape
    span = DIL_BLOCK * max(r for _, r in DIL_PATTERNS)
    assert t % span == 0 and len(DIL_PATTERNS) == 3
    nh, nbr = DIL_HEADS, len(DIL_PATTERNS)
    spec = lambda off: pl.BlockSpec((1, span, HEAD_DIM), lambda bi, h, s: (bi, s, off + h))
    return pl.pallas_call(
        functools.partial(_dilated_kernel, span=span),
        grid=(b, nh, t // span),
        in_specs=[spec(0), spec(nh), spec(2 * nh)],
        out_specs=pl.BlockSpec((1, span, HEAD_DIM), lambda bi, h, s: (bi, s, h)),
        out_shape=jax.ShapeDtypeStruct((b, t, nh * HEAD_DIM), BF16),
        scratch_shapes=[
            pltpu.VMEM((3, span, HEAD_DIM), F32),
            pltpu.VMEM((nbr, span, HEAD_DIM), BF16),
            pltpu.VMEM((nbr, 2, span, HEAD_DIM), BF16),
            pltpu.VMEM((nbr, 2, span, HEAD_DIM), BF16),
            pltpu.VMEM((nbr, 2, DIL_BLOCK, 2 * DIL_BLOCK), F32),
            pltpu.VMEM((nbr, span, HEAD_DIM), F32),
            pltpu.VMEM((nbr, span, HEAD_DIM), F32),
        ],
        compiler_params=_params(("parallel", "parallel", "arbitrary")),
        name="dilated_attention",
    )(qkv, qkv, qkv)


def _rmsnorm_kernel(x_ref, w_ref, o_ref):
    o_ref[...] = _rms(x_ref[...], w_ref[...])


def rmsnorm(x, w, tm=512):
    m, d = x.shape
    tm = min(tm, m)
    assert m % tm == 0
    return pl.pallas_call(
        _rmsnorm_kernel,
        grid=(m // tm,),
        in_specs=[pl.BlockSpec((tm, d), lambda i: (i, 0)), pl.BlockSpec((1, d), lambda i: (0, 0))],
        out_specs=pl.BlockSpec((tm, d), lambda i: (i, 0)),
        out_shape=jax.ShapeDtypeStruct((m, d), F32),
        compiler_params=_params(("parallel",)),
        name="rmsnorm",
    )(x, w.reshape(1, d))


def _even_mixer(h, bsz, seq, nw, ffn_nw, w_in, conv_w, conv_b, dt_bias, a_log, d_skip, ssm_norm_w, w_out):
    scale = LOG2_E * HEAD_DIM ** -0.5
    qkv_cols = 3 * SB_WIDTH
    w_qkv = jnp.concatenate([w_in[:, :SB_WIDTH] * scale, w_in[:, SB_WIDTH:qkv_cols]], axis=1).astype(BF16)
    w_zxd = jnp.pad(w_in[:, qkv_cols:], ((0, 0), (0, DT_PAD - SSM_HEADS))).astype(BF16)
    qkv = norm_matmul(h, nw, w_qkv, BF16)
    zxd = norm_matmul(h, nw, w_zxd, F32, tn=ZXD_WIDTH // 3)
    o_a = sb_attention(qkv.reshape(bsz, seq, qkv_cols)).reshape(bsz * seq, SB_WIDTH)
    o_b = ssd(zxd.reshape(bsz, seq, ZXD_WIDTH), conv_w, conv_b, dt_bias, a_log, d_skip,
              ssm_norm_w).reshape(bsz * seq, SSM_INNER)
    w_out = w_out.astype(BF16)
    return matmul_residual([o_a, o_b], [w_out[:SB_WIDTH], w_out[SB_WIDTH:]], h, ffn_nw)


def _odd_mixer(h, bsz, seq, nw, ffn_nw, w_in, w_out):
    scale = HEAD_DIM ** -0.5
    w_qkv = jnp.concatenate([w_in[:, :D_MODEL] * scale, w_in[:, D_MODEL:]], axis=1).astype(BF16)
    qkv = norm_matmul(h, nw, w_qkv, BF16)
    o = dilated_attention(qkv.reshape(bsz, seq, 3 * D_MODEL)).reshape(bsz * seq, D_MODEL)
    return matmul_residual([o], [w_out.astype(BF16)], h, ffn_nw)


def kernel(x, mix_norm_w, ffn_norm_w, final_norm_w, ev_w_in, ev_conv_w, ev_conv_b, ev_dt_bias, ev_a_log, ev_d_skip, ev_ssm_norm_w, ev_w_out, od_w_in, od_w_out, ffn_w_gate, ffn_w_up, ffn_conv_w, ffn_conv_b, ffn_w_down):
    bsz, seq, d = x.shape
    h = x.reshape(bsz * seq, d)
    for layer in range(DEPTH):
        i = layer // 2
        if layer % 2 == 0:
            h, xn = _even_mixer(h, bsz, seq, mix_norm_w[layer], ffn_norm_w[layer], ev_w_in[i], ev_conv_w[i],
                                ev_conv_b[i], ev_dt_bias[i], ev_a_log[i], ev_d_skip[i], ev_ssm_norm_w[i],
                                ev_w_out[i])
        else:
            h, xn = _odd_mixer(h, bsz, seq, mix_norm_w[layer], ffn_norm_w[layer], od_w_in[i], od_w_out[i])
        h = conv_ffn(xn, h, seq, ffn_w_gate[layer].astype(BF16), ffn_w_up[layer].astype(BF16),
                     ffn_conv_w[layer], ffn_conv_b[layer], ffn_w_down[layer].astype(BF16))
    return rmsnorm(h, final_norm_w).reshape(bsz, seq, d)
```

```python
import functools

import jax
import jax.numpy as jnp
from jax import lax
from jax.experimental import pallas as pl
from jax.experimental.pallas import tpu as pltpu

F32 = jnp.float32
BF16 = jnp.bfloat16

D_MODEL = 2048
DEPTH = 4
SB_HEADS = 8
HEAD_DIM = 128
SB_WIDTH = SB_HEADS * HEAD_DIM
SSM_HEAD_DIM = 64
SSM_INNER = 1024
SSM_HEADS = 16
SSM_GROUPS = 2
SSM_STATE = 128
SSM_CONV = 4
SSM_CHUNK = 128
CONV_DIM = SSM_INNER + 2 * SSM_GROUPS * SSM_STATE
DIL_HEADS = 16
DIL_PATTERNS = ((128, 1), (512, 4), (2048, 16))
DIL_BLOCK = 128
D_FF = 5632
FFN_CONV = 3
EPS = 1e-6

LANES = 128
SUBLANES = 8
VMEM_LIMIT = 56 * 1024 * 1024
DT_PAD = LANES
ZXD_WIDTH = SSM_INNER + CONV_DIM + DT_PAD

SB_EXIT_LOG2 = 160.0
LOG2_E = 1.4426950408889634
NEG_BIG = -1e30


def _params(sem, vmem=VMEM_LIMIT):
    return pltpu.CompilerParams(dimension_semantics=sem, vmem_limit_bytes=vmem)


def _rms(x, w):
    ms = jnp.mean(x * x, axis=-1, keepdims=True)
    return x * lax.rsqrt(ms + EPS) * w


def _sigmoid(x):
    return 1.0 / (1.0 + jnp.exp(-x))


def _shift_rows(x, halo, s):
    n = x.shape[0]
    r = pltpu.roll(x, s, axis=0)
    hr = pltpu.roll(halo, s, axis=0)
    rid = lax.broadcasted_iota(jnp.int32, hr.shape, 0)
    top = jnp.where(rid < s, hr, r[:SUBLANES])
    return jnp.concatenate([top, r[SUBLANES:]], axis=0) if n > SUBLANES else top


CAST_BLOCK_BYTES = 6 * 1024 * 1024
BF16_ROWS = 2 * SUBLANES


def _block_rows(total_rows, row_bytes):
    best = None
    for rows in range(BF16_ROWS, total_rows + 1, BF16_ROWS):
        if total_rows % rows == 0 and rows * row_bytes <= CAST_BLOCK_BYTES:
            best = rows
    assert best is not None
    return best


def _cast_kernel(w_ref, o_ref):
    o_ref[...] = w_ref[...].astype(BF16)


def _cast_scaled_kernel(w_ref, s_ref, o_ref):
    o_ref[...] = (w_ref[...] * s_ref[...]).astype(BF16)


def cast_bf16(w, colscale=None):
    shape = w.shape
    n = shape[-1]
    w2 = w.reshape(-1, n)
    total = w2.shape[0]
    rows = _block_rows(total, n * 4)
    row_spec = pl.BlockSpec((rows, n), lambda i: (i, 0))
    if colscale is None:
        body, in_specs, args = _cast_kernel, [row_spec], (w2,)
    else:
        body, in_specs = _cast_scaled_kernel, [row_spec, pl.BlockSpec((1, n), lambda i: (0, 0))]
        args = (w2, colscale.reshape(1, n))
    out = pl.pallas_call(
        body,
        grid=(total // rows,),
        in_specs=in_specs,
        out_specs=row_spec,
        out_shape=jax.ShapeDtypeStruct((total, n), BF16),
        compiler_params=_params(("parallel",)),
        name="cast_bf16",
    )(*args)
    return out.reshape(shape)


def _split_in_proj_kernel(w_ref, s_ref, qkv_ref, zxd_ref, *, qkv_cols, zx_cols, dt_cols):
    w = w_ref[...] * s_ref[...]
    qkv_ref[...] = w[:, :qkv_cols].astype(BF16)
    zxd_ref[:, :zx_cols] = w[:, qkv_cols:qkv_cols + zx_cols].astype(BF16)
    zxd_ref[:, zx_cols:] = jnp.zeros((w.shape[0], DT_PAD), BF16)
    zxd_ref[:, zx_cols:zx_cols + dt_cols] = w[:, qkv_cols + zx_cols:].astype(BF16)


def split_in_proj(w_in, colscale):
    s, d, n = w_in.shape
    qkv_cols = 3 * SB_WIDTH
    zx_cols = SSM_INNER + CONV_DIM
    assert n == qkv_cols + zx_cols + SSM_HEADS
    w2 = w_in.reshape(s * d, n)
    rows = _block_rows(s * d, n * 4)
    qkv, zxd = pl.pallas_call(
        functools.partial(_split_in_proj_kernel, qkv_cols=qkv_cols, zx_cols=zx_cols, dt_cols=SSM_HEADS),
        grid=(s * d // rows,),
        in_specs=[pl.BlockSpec((rows, n), lambda i: (i, 0)), pl.BlockSpec((1, n), lambda i: (0, 0))],
        out_specs=[pl.BlockSpec((rows, qkv_cols), lambda i: (i, 0)), pl.BlockSpec((rows, ZXD_WIDTH), lambda i: (i, 0))],
        out_shape=[jax.ShapeDtypeStruct((s * d, qkv_cols), BF16), jax.ShapeDtypeStruct((s * d, ZXD_WIDTH), BF16)],
        compiler_params=_params(("parallel",)),
        name="split_in_proj",
    )(w2, colscale.reshape(1, n))
    return qkv.reshape(s, d, qkv_cols), zxd.reshape(s, d, ZXD_WIDTH)


def _norm_matmul_kernel(x_ref, nw_ref, w_ref, o_ref, xn_ref):
    @pl.when(pl.program_id(1) == 0)
    def _():
        xn_ref[...] = _rms(x_ref[...], nw_ref[...]).astype(BF16)

    o_ref[...] = jnp.dot(xn_ref[...], w_ref[...], preferred_element_type=F32).astype(o_ref.dtype)


def norm_matmul(x, nw, w, layer, out_dtype, tm=1024, tn=1536):
    m, d = x.shape
    n = w.shape[2]
    tm, tn = min(tm, m), min(tn, n)
    assert m % tm == 0 and n % tn == 0
    return pl.pallas_call(
        _norm_matmul_kernel,
        grid=(m // tm, n // tn),
        in_specs=[
            pl.BlockSpec((tm, d), lambda i, j: (i, 0)),
            pl.BlockSpec((1, d), lambda i, j: (0, 0)),
            pl.BlockSpec((None, d, tn), lambda i, j: (layer, 0, j)),
        ],
        out_specs=pl.BlockSpec((tm, tn), lambda i, j: (i, j)),
        out_shape=jax.ShapeDtypeStruct((m, n), out_dtype),
        scratch_shapes=[pltpu.VMEM((tm, d), BF16)],
        compiler_params=_params(("parallel", "arbitrary")),
        name="norm_matmul",
    )(x, nw.reshape(1, d), w)


def _matmul_residual_kernel(*refs, n_in):
    xs, ws = refs[:n_in], refs[n_in:2 * n_in]
    h_ref, nw_ref, o_ref, xn_ref = refs[2 * n_in:]
    acc = h_ref[...]
    for x_ref, w_ref in zip(xs, ws):
        acc = acc + jnp.dot(x_ref[...], w_ref[...], preferred_element_type=F32)
    o_ref[...] = acc
    xn_ref[...] = _rms(acc, nw_ref[...]).astype(BF16)


def matmul_residual(xs, w, layer, h, nw, tm=512):
    m, n = h.shape
    tm = min(tm, m)
    kx = xs[0].shape[1]
    assert m % tm == 0 and all(x.shape[1] == kx for x in xs) and kx * len(xs) == w.shape[1]
    in_specs = [pl.BlockSpec((tm, kx), lambda i: (i, 0)) for _ in xs]
    in_specs += [pl.BlockSpec((None, kx, n), lambda i, p=p: (layer, p, 0)) for p in range(len(xs))]
    in_specs += [pl.BlockSpec((tm, n), lambda i: (i, 0)), pl.BlockSpec((1, n), lambda i: (0, 0))]
    return pl.pallas_call(
        functools.partial(_matmul_residual_kernel, n_in=len(xs)),
        grid=(m // tm,),
        in_specs=in_specs,
        out_specs=[pl.BlockSpec((tm, n), lambda i: (i, 0)), pl.BlockSpec((tm, n), lambda i: (i, 0))],
        out_shape=[jax.ShapeDtypeStruct((m, n), F32), jax.ShapeDtypeStruct((m, n), BF16)],
        compiler_params=_params(("parallel",)),
        name="matmul_residual",
    )(*xs, *([w] * len(xs)), h, nw.reshape(1, n))


def _conv_ffn_kernel(xn_ref, h_ref, wg_ref, wu_ref, cw_ref, cb_ref, wd_ref, o_ref, act_ref, carry_ref,
                     *, n_f, tf, tiles_per_seq):
    i, j = pl.program_id(0), pl.program_id(1)

    @pl.when(j < n_f)
    def _():
        @pl.when(i % tiles_per_seq == 0)
        def _():
            carry_ref[j] = jnp.zeros(carry_ref.shape[1:], F32)

        xn = xn_ref[...]
        g = jnp.dot(xn, wg_ref[...], preferred_element_type=F32)
        u = jnp.dot(xn, wu_ref[...], preferred_element_type=F32)
        halo = carry_ref[j]
        carry_ref[j] = g[-SUBLANES:]
        cw = cw_ref[...]
        c = cw[2:3] * g + cw[1:2] * _shift_rows(g, halo, 1) + cw[0:1] * _shift_rows(g, halo, 2) + cb_ref[...]
        act_ref[j] = (c * _sigmoid(c) * u).astype(BF16)

    @pl.when(j >= n_f)
    def _():
        acc = h_ref[...]
        for f in range(n_f):
            acc = acc + jnp.dot(act_ref[f], wd_ref[f * tf:(f + 1) * tf, :], preferred_element_type=F32)
        o_ref[...] = acc


def conv_ffn(xn, h, seq_len, wg, wu, cw, cb, wd, layer, tm=1024, tf=512, tn=512):
    m, d = h.shape
    dff = wg.shape[2]
    tm = min(tm, seq_len)
    assert seq_len % tm == 0 and dff % tf == 0 and m % seq_len == 0 and d % tn == 0
    n_f = dff // tf
    gate = lambda i, j: (layer, 0, jnp.minimum(j, n_f - 1))
    down = lambda i, j: (layer, 0, jnp.maximum(j - n_f, 0))
    out = lambda i, j: (i, jnp.maximum(j - n_f, 0))
    return pl.pallas_call(
        functools.partial(_conv_ffn_kernel, n_f=n_f, tf=tf, tiles_per_seq=seq_len // tm),
        grid=(m // tm, n_f + d // tn),
        in_specs=[
            pl.BlockSpec((tm, d), lambda i, j: (i, 0)),
            pl.BlockSpec((tm, tn), out),
            pl.BlockSpec((None, d, tf), gate),
            pl.BlockSpec((None, d, tf), gate),
            pl.BlockSpec((None, FFN_CONV, tf), gate),
            pl.BlockSpec((None, 1, tf), gate),
            pl.BlockSpec((None, dff, tn), down),
        ],
        out_specs=pl.BlockSpec((tm, tn), out),
        out_shape=jax.ShapeDtypeStruct((m, d), F32),
        scratch_shapes=[pltpu.VMEM((n_f, tm, tf), BF16), pltpu.VMEM((n_f, SUBLANES, tf), F32)],
        compiler_params=_params(("arbitrary", "arbitrary")),
        name="conv_ffn",
    )(xn, h, wg, wu, cw, cb.reshape(cb.shape[0], 1, dff), wd)


def _sb_kernel(q_ref, k_ref, v_ref, o_ref, *, tile, heads):
    qi = pl.program_id(2)
    row = lax.broadcasted_iota(jnp.int32, (tile, tile), 0)
    col = lax.broadcasted_iota(jnp.int32, (tile, tile), 1)
    later = (row > col).astype(BF16)
    strictly_before = col < row

    def logits(q, hs, j, diagonal):
        k = k_ref[0, pl.ds(pl.multiple_of(j * tile, tile), tile), hs]
        z = lax.dot_general(q, k, (((1,), (1,)), ((), ())), preferred_element_type=F32)
        sp = jnp.maximum(z, 0.0) + jnp.log2(1.0 + jnp.exp2(-jnp.abs(z)))
        log_keep = -sp
        if diagonal:
            log_keep = jnp.where(strictly_before, log_keep, 0.0)
        hi = log_keep.astype(BF16)
        lo = (log_keep - hi.astype(F32)).astype(BF16)
        suffix = (jnp.dot(hi, later, preferred_element_type=F32)
                  + jnp.dot(lo, later, preferred_element_type=F32))
        return (z - sp) + suffix, suffix[:, :1] + log_keep[:, :1]

    def weighted(hs, j, w):
        v = v_ref[0, pl.ds(pl.multiple_of(j * tile, tile), tile), hs]
        return jnp.dot(w.astype(BF16), v, preferred_element_type=F32)

    jb = jnp.maximum(qi - 1, 0)
    no_left_tile = jnp.where(qi > 0, 0.0, NEG_BIG)
    head_slices = [slice(hd * HEAD_DIM, (hd + 1) * HEAD_DIM) for hd in range(heads)]
    first_two = []
    for hs in head_slices:
        q = q_ref[0, :, hs]
        la, ta = logits(q, hs, qi, True)
        lb, tb = logits(q, hs, jb, False)
        acc = weighted(hs, qi, jnp.where(strictly_before, jnp.exp2(la), 0.0))
        acc = acc + weighted(hs, jb, jnp.exp2(lb + (ta + no_left_tile)))
        first_two.append((q, ta + tb, acc))

    for hs, (q, run, acc) in zip(head_slices, first_two):

        def cond(carry):
            j, alive, _, _ = carry
            return jnp.logical_and(j >= 0, alive > 0)

        def body(carry, q=q, hs=hs):
            j, _, run, acc = carry
            lj, tj = logits(q, hs, j, False)
            acc = acc + weighted(hs, j, jnp.exp2(lj + run))
            run = run + tj
            alive = (jnp.max(run) > -SB_EXIT_LOG2).astype(jnp.int32)
            return j - 1, alive, run, acc

        alive0 = (jnp.max(run) > -SB_EXIT_LOG2).astype(jnp.int32)
        _, _, _, acc = lax.while_loop(cond, body, (qi - 2, alive0, run, acc))
        o_ref[0, :, hs] = acc.astype(o_ref.dtype)


def sb_attention(qkv, tile=256, heads=2):
    b, t, _ = qkv.shape
    tile = min(tile, t)
    assert t % tile == 0 and SB_HEADS % heads == 0
    groups = SB_HEADS // heads
    width = heads * HEAD_DIM
    return pl.pallas_call(
        functools.partial(_sb_kernel, tile=tile, heads=heads),
        grid=(b, groups, t // tile),
        in_specs=[
            pl.BlockSpec((1, tile, width), lambda bi, g, i: (bi, i, g)),
            pl.BlockSpec((1, t, width), lambda bi, g, i: (bi, 0, groups + g)),
            pl.BlockSpec((1, t, width), lambda bi, g, i: (bi, 0, 2 * groups + g)),
        ],
        out_specs=pl.BlockSpec((1, tile, width), lambda bi, g, i: (bi, i, g)),
        out_shape=jax.ShapeDtypeStruct((b, t, SB_WIDTH), BF16),
        compiler_params=_params(("parallel", "parallel", "arbitrary")),
        name="sb_attention",
    )(qkv, qkv, qkv)


def _ssd_kernel(zxd_ref, halo_ref, cw_ref, cb_ref, dtb_ref, alog_ref, dskip_ref, nw_ref, o_ref, state_ref):
    c = pl.program_id(1)
    L, N, P = SSM_CHUNK, SSM_STATE, SSM_HEAD_DIM
    n_pairs = SSM_HEADS // 2
    pairs_per_group = n_pairs // SSM_GROUPS

    @pl.when(c == 0)
    def _():
        state_ref[...] = jnp.zeros_like(state_ref)

    blk = zxd_ref[0]
    z = blk[:, :SSM_INNER]
    xbc = blk[:, SSM_INNER:SSM_INNER + CONV_DIM]
    dt_raw = blk[:, SSM_INNER + CONV_DIM:]
    halo = jnp.where(c == 0, 0.0, halo_ref[0][:, SSM_INNER:SSM_INNER + CONV_DIM])

    cw = cw_ref[...]
    conv = cw[3:4] * xbc + cb_ref[...]
    for s in range(1, SSM_CONV):
        conv = conv + cw[3 - s:4 - s] * _shift_rows(xbc, halo, s)
    xbc = conv * _sigmoid(conv)
    xs = xbc[:, :SSM_INNER]
    b_in = xbc[:, SSM_INNER:SSM_INNER + SSM_GROUPS * N]
    c_in = xbc[:, SSM_INNER + SSM_GROUPS * N:]

    pre = dt_raw + dtb_ref[...]
    dt = jnp.maximum(pre, 0.0) + jnp.log1p(jnp.exp(-jnp.abs(pre)))
    a = -jnp.exp(alog_ref[...])
    row = lax.broadcasted_iota(jnp.int32, (L, L), 0)
    col = lax.broadcasted_iota(jnp.int32, (L, L), 1)
    causal = row >= col
    tri = causal.astype(F32)
    a_cs = jnp.dot(tri, dt * a, preferred_element_type=F32, precision=lax.Precision.HIGHEST)
    a_cs_t = a_cs.T
    a_end = a_cs[L - 1:L, :]

    lane = lax.broadcasted_iota(jnp.int32, (L, LANES), 1)
    first_half = lane < P
    srow = lax.broadcasted_iota(jnp.int32, (2 * N, LANES), 0)
    slane = lax.broadcasted_iota(jnp.int32, (2 * N, LANES), 1)
    block_diag = (srow < N) == (slane < P)

    scores = []
    for g in range(SSM_GROUPS):
        cg = c_in[:, g * N:(g + 1) * N].astype(BF16)
        bg = b_in[:, g * N:(g + 1) * N].astype(BF16)
        scores.append(lax.dot_general(cg, bg, (((1,), (1,)), ((), ())), preferred_element_type=F32))

    ys = []
    for j in range(n_pairs):
        g = j // pairs_per_group
        h0, h1 = 2 * j, 2 * j + 1
        cg = c_in[:, g * N:(g + 1) * N]
        bg = b_in[:, g * N:(g + 1) * N]
        xs_pair = xs[:, j * LANES:(j + 1) * LANES]
        dt_pair = jnp.where(first_half, dt[:, h0:h0 + 1], dt[:, h1:h1 + 1])
        xdt = xs_pair * dt_pair
        sd, c_dec, b_dec, cd = [], [], [], []
        for h in (h0, h1):
            col_h = a_cs[:, h:h + 1]
            row_h = a_cs_t[h:h + 1, :]
            decay = jnp.where(causal, jnp.exp(col_h - row_h), 0.0)
            sd.append((scores[g] * decay).astype(BF16))
            c_dec.append((cg * jnp.exp(col_h)).astype(BF16))
            b_dec.append(bg * jnp.exp(a_end[:, h:h + 1] - col_h))
            cd.append(jnp.exp(a_end[:, h:h + 1]))
        xdt_bf = xdt.astype(BF16)
        zero = jnp.zeros_like(xdt_bf)
        rhs = jnp.concatenate([jnp.where(first_half, xdt_bf, zero), jnp.where(first_half, zero, xdt_bf)], axis=0)
        y = jnp.dot(jnp.concatenate(sd, axis=1), rhs, preferred_element_type=F32)
        state = state_ref[j]
        y = y + jnp.dot(jnp.concatenate(c_dec, axis=1), state.astype(BF16), preferred_element_type=F32)
        b_pair_t = jnp.concatenate(b_dec, axis=1).T.astype(BF16)
        upd = jnp.dot(b_pair_t, xdt_bf, preferred_element_type=F32)
        cd_rows = jnp.where(srow < N, cd[0], cd[1])
        state_ref[j] = jnp.where(block_diag, state * cd_rows + upd, 0.0)
        ys.append(y + dskip_ref[:, j * LANES:(j + 1) * LANES] * xs_pair)

    y = jnp.concatenate(ys, axis=1) * (z * _sigmoid(z))
    o_ref[0] = _rms(y, nw_ref[...]).astype(o_ref.dtype)


def ssd(zxd, cw, cb, dt_bias, a_log, d_skip, norm_w):
    b, t, width = zxd.shape
    L = SSM_CHUNK
    assert t % L == 0 and width == ZXD_WIDTH
    pad = lambda p: jnp.pad(p.astype(F32), (0, DT_PAD - SSM_HEADS)).reshape(1, DT_PAD)
    dskip_lanes = jnp.repeat(d_skip.astype(F32), SSM_HEAD_DIM).reshape(1, SSM_INNER)
    full = lambda shape: pl.BlockSpec(shape, lambda bi, c: (0,) * len(shape))
    return pl.pallas_call(
        _ssd_kernel,
        grid=(b, t // L),
        in_specs=[
            pl.BlockSpec((1, L, width), lambda bi, c: (bi, c, 0)),
            pl.BlockSpec((1, SUBLANES, width), lambda bi, c: (bi, jnp.maximum(c * (L // SUBLANES) - 1, 0), 0)),
            full((SSM_CONV, CONV_DIM)), full((1, CONV_DIM)), full((1, DT_PAD)), full((1, DT_PAD)),
            full((1, SSM_INNER)), full((1, SSM_INNER)),
        ],
        out_specs=pl.BlockSpec((1, L, SSM_INNER), lambda bi, c: (bi, c, 0)),
        out_shape=jax.ShapeDtypeStruct((b, t, SSM_INNER), BF16),
        scratch_shapes=[pltpu.VMEM((SSM_HEADS // 2, 2 * SSM_STATE, LANES), F32)],
        compiler_params=_params(("parallel", "arbitrary")),
        name="ssd",
    )(zxd, zxd, cw, cb.reshape(1, CONV_DIM), pad(dt_bias), pad(a_log), dskip_lanes, norm_w.reshape(1, SSM_INNER))


def _dilated_kernel(q_ref, k_ref, v_ref, o_ref, xf, qd, kd, vd, bias, of, lf, *, span):
    si = pl.program_id(2)
    blk = DIL_BLOCK
    units = span // blk
    cur = si % 2
    prv = 1 - cur

    @pl.when(si == 0)
    def _():
        kd[:, 1] = jnp.zeros(kd.shape[:1] + kd.shape[2:], kd.dtype)
        vd[:, 1] = jnp.zeros(vd.shape[:1] + vd.shape[2:], vd.dtype)
        qpos = lax.broadcasted_iota(jnp.int32, (blk, 2 * blk), 0)
        kpos = lax.broadcasted_iota(jnp.int32, (blk, 2 * blk), 1)
        dist = qpos + blk - kpos
        for bi, (window, r) in enumerate(DIL_PATTERNS):
            band = jnp.logical_and(dist >= 0, dist <= window // r)
            bias[bi, 0] = jnp.where(band, 0.0, NEG_BIG)
            bias[bi, 1] = jnp.where(jnp.logical_and(band, kpos >= blk), 0.0, NEG_BIG)

    xf[0] = q_ref[0].astype(F32)
    xf[1] = k_ref[0].astype(F32)
    xf[2] = v_ref[0].astype(F32)
    for bi, (_, r) in enumerate(DIL_PATTERNS):
        rows = span // r
        if r == 1:
            kd[bi, cur] = k_ref[0]
            vd[bi, cur] = v_ref[0]
            continue
        for c in range(r):
            src = pl.ds(c, rows, stride=r)
            dst = pl.ds(c * rows, rows)
            qd[bi, dst, :] = xf[0, src, :].astype(BF16)
            kd[bi, cur, dst, :] = xf[1, src, :].astype(BF16)
            vd[bi, cur, dst, :] = xf[2, src, :].astype(BF16)

    first = jnp.where(si == 0, 1, 0)
    ones = jnp.ones((2 * blk, HEAD_DIM), BF16)
    for bi, (_, r) in enumerate(DIL_PATTERNS):
        rows = span // r
        for u in range(units):
            cls, nb = divmod(u, units // r)
            base = cls * rows + nb * blk
            here = pl.ds(base, blk)
            q = q_ref[0, here, :] if r == 1 else qd[bi, here, :]
            if nb > 0:
                before = pl.ds(base - blk, blk)
                k_prev, v_prev = kd[bi, cur, before, :], vd[bi, cur, before, :]
                b = bias[bi, 0]
            else:
                before = pl.ds(cls * rows + rows - blk, blk)
                k_prev, v_prev = kd[bi, prv, before, :], vd[bi, prv, before, :]
                b = bias[bi, first]
            k = jnp.concatenate([k_prev, kd[bi, cur, here, :]], axis=0)
            v = jnp.concatenate([v_prev, vd[bi, cur, here, :]], axis=0)
            s = lax.dot_general(q, k, (((1,), (1,)), ((), ())), preferred_element_type=F32) + b
            m = jnp.max(s, axis=-1, keepdims=True)
            p = jnp.exp2(s - m).astype(BF16)
            pv = jnp.dot(p, jnp.concatenate([v, ones], axis=1), preferred_element_type=F32)
            den = pv[:, HEAD_DIM:]
            o = pv[:, :HEAD_DIM] / den
            lse = m + jnp.log2(den)
            natural = pl.ds(base, blk) if r == 1 else pl.ds(cls + r * blk * nb, blk, stride=r)
            of[bi, natural, :] = o
            lf[bi, natural, :] = lse

    l0, l1, l2 = lf[0], lf[1], lf[2]
    m = jnp.maximum(jnp.maximum(l0, l1), l2)
    e0, e1, e2 = jnp.exp2(l0 - m), jnp.exp2(l1 - m), jnp.exp2(l2 - m)
    out = (e0 * of[0] + e1 * of[1] + e2 * of[2]) / (e0 + e1 + e2)
    o_ref[0] = out.astype(o_ref.dtype)


def dilated_attention(qkv):
    b, t, _ = qkv.shape
    span = DIL_BLOCK * max(r for _, r in DIL_PATTERNS)
    assert t % span == 0 and len(DIL_PATTERNS) == 3
    nh, nbr = DIL_HEADS, len(DIL_PATTERNS)
    spec = lambda off: pl.BlockSpec((1, span, HEAD_DIM), lambda bi, h, s: (bi, s, off + h))
    return pl.pallas_call(
        functools.partial(_dilated_kernel, span=span),
        grid=(b, nh, t // span),
        in_specs=[spec(0), spec(nh), spec(2 * nh)],
        out_specs=pl.BlockSpec((1, span, HEAD_DIM), lambda bi, h, s: (bi, s, h)),
        out_shape=jax.ShapeDtypeStruct((b, t, nh * HEAD_DIM), BF16),
        scratch_shapes=[
            pltpu.VMEM((3, span, HEAD_DIM), F32),
            pltpu.VMEM((nbr, span, HEAD_DIM), BF16),
            pltpu.VMEM((nbr, 2, span, HEAD_DIM), BF16),
            pltpu.VMEM((nbr, 2, span, HEAD_DIM), BF16),
            pltpu.VMEM((nbr, 2, DIL_BLOCK, 2 * DIL_BLOCK), F32),
            pltpu.VMEM((nbr, span, HEAD_DIM), F32),
            pltpu.VMEM((nbr, span, HEAD_DIM), F32),
        ],
        compiler_params=_params(("parallel", "parallel", "arbitrary")),
        name="dilated_attention",
    )(qkv, qkv, qkv)


def _rmsnorm_kernel(x_ref, w_ref, o_ref):
    o_ref[...] = _rms(x_ref[...], w_ref[...])


def rmsnorm(x, w, tm=512):
    m, d = x.shape
    tm = min(tm, m)
    assert m % tm == 0
    return pl.pallas_call(
        _rmsnorm_kernel,
        grid=(m // tm,),
        in_specs=[pl.BlockSpec((tm, d), lambda i: (i, 0)), pl.BlockSpec((1, d), lambda i: (0, 0))],
        out_specs=pl.BlockSpec((tm, d), lambda i: (i, 0)),
        out_shape=jax.ShapeDtypeStruct((m, d), F32),
        compiler_params=_params(("parallel",)),
        name="rmsnorm",
    )(x, w.reshape(1, d))


def _q_colscale(n_cols, q_cols):
    return jnp.where(jnp.arange(n_cols) < q_cols, LOG2_E * HEAD_DIM ** -0.5, 1.0).astype(F32)


def _even_mixer(h, bsz, seq, i, nw, ffn_nw, w_qkv, w_zxd, conv_w, conv_b, dt_bias, a_log, d_skip, ssm_norm_w, w_out):
    qkv = norm_matmul(h, nw, w_qkv, i, BF16)
    zxd = norm_matmul(h, nw, w_zxd, i, F32, tn=ZXD_WIDTH // 3)
    o_a = sb_attention(qkv.reshape(bsz, seq, 3 * SB_WIDTH)).reshape(bsz * seq, SB_WIDTH)
    o_b = ssd(zxd.reshape(bsz, seq, ZXD_WIDTH), conv_w, conv_b, dt_bias, a_log, d_skip,
              ssm_norm_w).reshape(bsz * seq, SSM_INNER)
    return matmul_residual([o_a, o_b], w_out, i, h, ffn_nw)


def _odd_mixer(h, bsz, seq, i, nw, ffn_nw, w_qkv, w_out):
    qkv = norm_matmul(h, nw, w_qkv, i, BF16)
    o = dilated_attention(qkv.reshape(bsz, seq, 3 * D_MODEL)).reshape(bsz * seq, D_MODEL)
    return matmul_residual([o], w_out, i, h, ffn_nw)


def kernel(x, mix_norm_w, ffn_norm_w, final_norm_w, ev_w_in, ev_conv_w, ev_conv_b, ev_dt_bias, ev_a_log, ev_d_skip, ev_ssm_norm_w, ev_w_out, od_w_in, od_w_out, ffn_w_gate, ffn_w_up, ffn_conv_w, ffn_conv_b, ffn_w_down):
    bsz, seq, d = x.shape
    ev_qkv, ev_zxd = split_in_proj(ev_w_in, _q_colscale(ev_w_in.shape[-1], SB_WIDTH))
    od_qkv = cast_bf16(od_w_in, _q_colscale(od_w_in.shape[-1], D_MODEL))
    ev_out, od_out = cast_bf16(ev_w_out), cast_bf16(od_w_out)
    w_gate, w_up, w_down = cast_bf16(ffn_w_gate), cast_bf16(ffn_w_up), cast_bf16(ffn_w_down)
    h = x.reshape(bsz * seq, d)
    for layer in range(DEPTH):
        i = layer // 2
        if layer % 2 == 0:
            h, xn = _even_mixer(h, bsz, seq, i, mix_norm_w[layer], ffn_norm_w[layer], ev_qkv, ev_zxd,
                                ev_conv_w[i], ev_conv_b[i], ev_dt_bias[i], ev_a_log[i], ev_d_skip[i],
                                ev_ssm_norm_w[i], ev_out)
        else:
            h, xn = _odd_mixer(h, bsz, seq, i, mix_norm_w[layer], ffn_norm_w[layer], od_qkv, od_out)
        h = conv_ffn(xn, h, seq, w_gate, w_up, ffn_conv_w, ffn_conv_b, w_down, layer)
    return rmsnorm(h, final_norm_w).reshape(bsz, seq, d)
```

```python
import functools

import jax
import jax.numpy as jnp
from jax import lax
from jax.experimental import pallas as pl
from jax.experimental.pallas import tpu as pltpu

F32 = jnp.float32
BF16 = jnp.bfloat16

D_MODEL = 2048
DEPTH = 4
SB_HEADS = 8
HEAD_DIM = 128
SB_WIDTH = SB_HEADS * HEAD_DIM
SSM_HEAD_DIM = 64
SSM_INNER = 1024
SSM_HEADS = 16
SSM_GROUPS = 2
SSM_STATE = 128
SSM_CONV = 4
SSM_CHUNK = 128
CONV_DIM = SSM_INNER + 2 * SSM_GROUPS * SSM_STATE
DIL_HEADS = 16
DIL_PATTERNS = ((128, 1), (512, 4), (2048, 16))
DIL_BLOCK = 128
D_FF = 5632
FFN_CONV = 3
EPS = 1e-6

LANES = 128
SUBLANES = 8
VMEM_LIMIT = 56 * 1024 * 1024
DT_PAD = LANES
ZXD_WIDTH = SSM_INNER + CONV_DIM + DT_PAD

SB_EXIT_LOG2 = 160.0
LOG2_E = 1.4426950408889634
NEG_BIG = -1e30


def _params(sem, vmem=VMEM_LIMIT):
    return pltpu.CompilerParams(dimension_semantics=sem, vmem_limit_bytes=vmem)


def _rms(x, w):
    ms = jnp.mean(x * x, axis=-1, keepdims=True)
    return x * lax.rsqrt(ms + EPS) * w


def _sigmoid(x):
    return 1.0 / (1.0 + jnp.exp(-x))


def _shift_rows(x, halo, s):
    n = x.shape[0]
    r = pltpu.roll(x, s, axis=0)
    hr = pltpu.roll(halo, s, axis=0)
    rid = lax.broadcasted_iota(jnp.int32, hr.shape, 0)
    top = jnp.where(rid < s, hr, r[:SUBLANES])
    return jnp.concatenate([top, r[SUBLANES:]], axis=0) if n > SUBLANES else top


CAST_BLOCK_BYTES = 6 * 1024 * 1024
BF16_ROWS = 2 * SUBLANES


def _block_rows(total_rows, row_bytes):
    best = None
    for rows in range(BF16_ROWS, total_rows + 1, BF16_ROWS):
        if total_rows % rows == 0 and rows * row_bytes <= CAST_BLOCK_BYTES:
            best = rows
    assert best is not None
    return best


def _cast_kernel(w_ref, o_ref):
    o_ref[...] = w_ref[...].astype(BF16)


def _cast_scaled_kernel(w_ref, s_ref, o_ref):
    o_ref[...] = (w_ref[...] * s_ref[...]).astype(BF16)


def cast_bf16(w, colscale=None):
    shape = w.shape
    n = shape[-1]
    w2 = w.reshape(-1, n)
    total = w2.shape[0]
    rows = _block_rows(total, n * 4)
    row_spec = pl.BlockSpec((rows, n), lambda i: (i, 0))
    if colscale is None:
        body, in_specs, args = _cast_kernel, [row_spec], (w2,)
    else:
        body, in_specs = _cast_scaled_kernel, [row_spec, pl.BlockSpec((1, n), lambda i: (0, 0))]
        args = (w2, colscale.reshape(1, n))
    out = pl.pallas_call(
        body,
        grid=(total // rows,),
        in_specs=in_specs,
        out_specs=row_spec,
        out_shape=jax.ShapeDtypeStruct((total, n), BF16),
        compiler_params=_params(("parallel",)),
        name="cast_bf16",
    )(*args)
    return out.reshape(shape)


def _split_in_proj_kernel(w_ref, s_ref, qkv_ref, zxd_ref, *, qkv_cols, zx_cols, dt_cols):
    w = w_ref[...] * s_ref[...]
    qkv_ref[...] = w[:, :qkv_cols].astype(BF16)
    zxd_ref[:, :zx_cols] = w[:, qkv_cols:qkv_cols + zx_cols].astype(BF16)
    zxd_ref[:, zx_cols:] = jnp.zeros((w.shape[0], DT_PAD), BF16)
    zxd_ref[:, zx_cols:zx_cols + dt_cols] = w[:, qkv_cols + zx_cols:].astype(BF16)


def split_in_proj(w_in, colscale):
    s, d, n = w_in.shape
    qkv_cols = 3 * SB_WIDTH
    zx_cols = SSM_INNER + CONV_DIM
    assert n == qkv_cols + zx_cols + SSM_HEADS
    rows = _block_rows(d, n * 4)
    block = lambda width: pl.BlockSpec((None, rows, width), lambda si, i: (si, i, 0))
    return pl.pallas_call(
        functools.partial(_split_in_proj_kernel, qkv_cols=qkv_cols, zx_cols=zx_cols, dt_cols=SSM_HEADS),
        grid=(s, d // rows),
        in_specs=[block(n), pl.BlockSpec((1, n), lambda si, i: (0, 0))],
        out_specs=[block(qkv_cols), block(ZXD_WIDTH)],
        out_shape=[jax.ShapeDtypeStruct((s, d, qkv_cols), BF16), jax.ShapeDtypeStruct((s, d, ZXD_WIDTH), BF16)],
        compiler_params=_params(("parallel", "parallel")),
        name="split_in_proj",
    )(w_in, colscale.reshape(1, n))


def _norm_matmul_kernel(x_ref, nw_ref, w_ref, o_ref, xn_ref):
    @pl.when(pl.program_id(1) == 0)
    def _():
        xn_ref[...] = _rms(x_ref[...], nw_ref[...]).astype(BF16)

    o_ref[...] = jnp.dot(xn_ref[...], w_ref[...], preferred_element_type=F32).astype(o_ref.dtype)


def norm_matmul(x, nw, w, layer, out_dtype, tm=1024, tn=1536):
    m, d = x.shape
    n = w.shape[2]
    tm, tn = min(tm, m), min(tn, n)
    assert m % tm == 0 and n % tn == 0
    return pl.pallas_call(
        _norm_matmul_kernel,
        grid=(m // tm, n // tn),
        in_specs=[
            pl.BlockSpec((tm, d), lambda i, j: (i, 0)),
            pl.BlockSpec((1, d), lambda i, j: (0, 0)),
            pl.BlockSpec((None, d, tn), lambda i, j: (layer, 0, j)),
        ],
        out_specs=pl.BlockSpec((tm, tn), lambda i, j: (i, j)),
        out_shape=jax.ShapeDtypeStruct((m, n), out_dtype),
        scratch_shapes=[pltpu.VMEM((tm, d), BF16)],
        compiler_params=_params(("parallel", "arbitrary")),
        name="norm_matmul",
    )(x, nw.reshape(1, d), w)


def _matmul_residual_kernel(*refs, n_in):
    xs, ws = refs[:n_in], refs[n_in:2 * n_in]
    h_ref, nw_ref, o_ref, xn_ref = refs[2 * n_in:]
    acc = h_ref[...]
    for x_ref, w_ref in zip(xs, ws):
        acc = acc + jnp.dot(x_ref[...], w_ref[...], preferred_element_type=F32)
    o_ref[...] = acc
    xn_ref[...] = _rms(acc, nw_ref[...]).astype(BF16)


def matmul_residual(xs, w, layer, h, nw, tm=512):
    m, n = h.shape
    tm = min(tm, m)
    kx = xs[0].shape[1]
    assert m % tm == 0 and all(x.shape[1] == kx for x in xs) and kx * len(xs) == w.shape[1]
    in_specs = [pl.BlockSpec((tm, kx), lambda i: (i, 0)) for _ in xs]
    in_specs += [pl.BlockSpec((None, kx, n), lambda i, p=p: (layer, p, 0)) for p in range(len(xs))]
    in_specs += [pl.BlockSpec((tm, n), lambda i: (i, 0)), pl.BlockSpec((1, n), lambda i: (0, 0))]
    return pl.pallas_call(
        functools.partial(_matmul_residual_kernel, n_in=len(xs)),
        grid=(m // tm,),
        in_specs=in_specs,
        out_specs=[pl.BlockSpec((tm, n), lambda i: (i, 0)), pl.BlockSpec((tm, n), lambda i: (i, 0))],
        out_shape=[jax.ShapeDtypeStruct((m, n), F32), jax.ShapeDtypeStruct((m, n), BF16)],
        compiler_params=_params(("parallel",)),
        name="matmul_residual",
    )(*xs, *([w] * len(xs)), h, nw.reshape(1, n))


def _conv_ffn_kernel(xn_ref, h_ref, wg_ref, wu_ref, cw_ref, cb_ref, wd_ref, o_ref, act_ref, carry_ref,
                     *, n_f, tf, tiles_per_seq):
    i, j = pl.program_id(0), pl.program_id(1)

    @pl.when(j < n_f)
    def _():
        @pl.when(i % tiles_per_seq == 0)
        def _():
            carry_ref[j] = jnp.zeros(carry_ref.shape[1:], F32)

        xn = xn_ref[...]
        g = jnp.dot(xn, wg_ref[...], preferred_element_type=F32)
        u = jnp.dot(xn, wu_ref[...], preferred_element_type=F32)
        halo = carry_ref[j]
        carry_ref[j] = g[-SUBLANES:]
        cw = cw_ref[...]
        c = cw[2:3] * g + cw[1:2] * _shift_rows(g, halo, 1) + cw[0:1] * _shift_rows(g, halo, 2) + cb_ref[...]
        act_ref[j] = (c * _sigmoid(c) * u).astype(BF16)

    @pl.when(j >= n_f)
    def _():
        acc = h_ref[...]
        for f in range(n_f):
            acc = acc + jnp.dot(act_ref[f], wd_ref[f * tf:(f + 1) * tf, :], preferred_element_type=F32)
        o_ref[...] = acc


def conv_ffn(xn, h, seq_len, wg, wu, cw, cb, wd, layer, tm=1024, tf=512, tn=512):
    m, d = h.shape
    dff = wg.shape[2]
    tm = min(tm, seq_len)
    assert seq_len % tm == 0 and dff % tf == 0 and m % seq_len == 0 and d % tn == 0
    n_f = dff // tf
    gate = lambda i, j: (layer, 0, jnp.minimum(j, n_f - 1))
    down = lambda i, j: (layer, 0, jnp.maximum(j - n_f, 0))
    out = lambda i, j: (i, jnp.maximum(j - n_f, 0))
    return pl.pallas_call(
        functools.partial(_conv_ffn_kernel, n_f=n_f, tf=tf, tiles_per_seq=seq_len // tm),
        grid=(m // tm, n_f + d // tn),
        in_specs=[
            pl.BlockSpec((tm, d), lambda i, j: (i, 0)),
            pl.BlockSpec((tm, tn), out),
            pl.BlockSpec((None, d, tf), gate),
            pl.BlockSpec((None, d, tf), gate),
            pl.BlockSpec((None, FFN_CONV, tf), gate),
            pl.BlockSpec((None, 1, tf), gate),
            pl.BlockSpec((None, dff, tn), down),
        ],
        out_specs=pl.BlockSpec((tm, tn), out),
        out_shape=jax.ShapeDtypeStruct((m, d), F32),
        scratch_shapes=[pltpu.VMEM((n_f, tm, tf), BF16), pltpu.VMEM((n_f, SUBLANES, tf), F32)],
        compiler_params=_params(("arbitrary", "arbitrary")),
        name="conv_ffn",
    )(xn, h, wg, wu, cw, cb.reshape(cb.shape[0], 1, dff), wd)


def _sb_kernel(q_ref, k_ref, v_ref, o_ref, *, tile, heads):
    qi = pl.program_id(2)
    row = lax.broadcasted_iota(jnp.int32, (tile, tile), 0)
    col = lax.broadcasted_iota(jnp.int32, (tile, tile), 1)
    later = (row > col).astype(BF16)
    later2 = jnp.concatenate([later, later], axis=0)
    strictly_before = col < row

    def logits(q, hs, j, diagonal):
        k = k_ref[0, pl.ds(pl.multiple_of(j * tile, tile), tile), hs]
        z = lax.dot_general(q, k, (((1,), (1,)), ((), ())), preferred_element_type=F32)
        sp = jnp.maximum(z, 0.0) + jnp.log2(1.0 + jnp.exp2(-jnp.abs(z)))
        log_keep = -sp
        if diagonal:
            log_keep = jnp.where(strictly_before, log_keep, 0.0)
        hi = log_keep.astype(BF16)
        lo = (log_keep - hi.astype(F32)).astype(BF16)
        suffix = jnp.dot(jnp.concatenate([hi, lo], axis=1), later2, preferred_element_type=F32)
        return (z - sp) + suffix, suffix[:, :1] + log_keep[:, :1]

    def weighted(hs, j, w):
        v = v_ref[0, pl.ds(pl.multiple_of(j * tile, tile), tile), hs]
        return jnp.dot(w.astype(BF16), v, preferred_element_type=F32)

    jb = jnp.maximum(qi - 1, 0)
    no_left_tile = jnp.where(qi > 0, 0.0, NEG_BIG)
    head_slices = [slice(hd * HEAD_DIM, (hd + 1) * HEAD_DIM) for hd in range(heads)]
    first_two = []
    for hs in head_slices:
        q = q_ref[0, :, hs]
        la, ta = logits(q, hs, qi, True)
        lb, tb = logits(q, hs, jb, False)
        acc = weighted(hs, qi, jnp.where(strictly_before, jnp.exp2(la), 0.0))
        acc = acc + weighted(hs, jb, jnp.exp2(lb + (ta + no_left_tile)))
        first_two.append((q, ta + tb, acc))

    for hs, (q, run, acc) in zip(head_slices, first_two):

        def cond(carry):
            j, alive, _, _ = carry
            return jnp.logical_and(j >= 0, alive > 0)

        def body(carry, q=q, hs=hs):
            j, _, run, acc = carry
            lj, tj = logits(q, hs, j, False)
            acc = acc + weighted(hs, j, jnp.exp2(lj + run))
            run = run + tj
            alive = (jnp.max(run) > -SB_EXIT_LOG2).astype(jnp.int32)
            return j - 1, alive, run, acc

        alive0 = (jnp.max(run) > -SB_EXIT_LOG2).astype(jnp.int32)
        _, _, _, acc = lax.while_loop(cond, body, (qi - 2, alive0, run, acc))
        o_ref[0, :, hs] = acc.astype(o_ref.dtype)


def sb_attention(qkv, tile=256, heads=2):
    b, t, _ = qkv.shape
    tile = min(tile, t)
    assert t % tile == 0 and SB_HEADS % heads == 0
    groups = SB_HEADS // heads
    width = heads * HEAD_DIM
    return pl.pallas_call(
        functools.partial(_sb_kernel, tile=tile, heads=heads),
        grid=(b, groups, t // tile),
        in_specs=[
            pl.BlockSpec((1, tile, width), lambda bi, g, i: (bi, i, g)),
            pl.BlockSpec((1, t, width), lambda bi, g, i: (bi, 0, groups + g)),
            pl.BlockSpec((1, t, width), lambda bi, g, i: (bi, 0, 2 * groups + g)),
        ],
        out_specs=pl.BlockSpec((1, tile, width), lambda bi, g, i: (bi, i, g)),
        out_shape=jax.ShapeDtypeStruct((b, t, SB_WIDTH), BF16),
        compiler_params=_params(("parallel", "parallel", "arbitrary")),
        name="sb_attention",
    )(qkv, qkv, qkv)


def _ssd_kernel(zxd_ref, halo_ref, cw_ref, cb_ref, dtb_ref, alog_ref, dskip_ref, nw_ref, expand_ref, o_ref, state_ref):
    c = pl.program_id(1)
    L, N, P = SSM_CHUNK, SSM_STATE, SSM_HEAD_DIM
    heads_per_group = SSM_HEADS // SSM_GROUPS
    gw = heads_per_group * P

    @pl.when(c == 0)
    def _():
        state_ref[...] = jnp.zeros_like(state_ref)

    blk = zxd_ref[0]
    z = blk[:, :SSM_INNER]
    xbc = blk[:, SSM_INNER:SSM_INNER + CONV_DIM]
    dt_raw = blk[:, SSM_INNER + CONV_DIM:]
    halo = jnp.where(c == 0, 0.0, halo_ref[0][:, SSM_INNER:SSM_INNER + CONV_DIM])

    cw = cw_ref[...]
    conv = cw[3:4] * xbc + cb_ref[...]
    for s in range(1, SSM_CONV):
        conv = conv + cw[3 - s:4 - s] * _shift_rows(xbc, halo, s)
    xbc = conv * _sigmoid(conv)
    xs = xbc[:, :SSM_INNER]
    b_in = xbc[:, SSM_INNER:SSM_INNER + SSM_GROUPS * N]
    c_in = xbc[:, SSM_INNER + SSM_GROUPS * N:]

    pre = dt_raw + dtb_ref[...]
    dt = jnp.maximum(pre, 0.0) + jnp.log1p(jnp.exp(-jnp.abs(pre)))
    a = -jnp.exp(alog_ref[...])
    row = lax.broadcasted_iota(jnp.int32, (L, L), 0)
    col = lax.broadcasted_iota(jnp.int32, (L, L), 1)
    causal = row >= col
    tri = causal.astype(F32)
    a_cs = jnp.dot(tri, dt * a, preferred_element_type=F32, precision=lax.Precision.HIGHEST)
    a_cs_t = a_cs.T
    a_end = a_cs[L - 1:L, :]

    expand = expand_ref[...]
    spread = lambda v: jnp.dot(v.astype(BF16), expand[:LANES], preferred_element_type=F32)
    dt_l = spread(dt)
    decay_in_l = spread(jnp.exp(a_cs))
    decay_out_l = spread(jnp.exp(a_end - a_cs))
    cd = jnp.broadcast_to(jnp.exp(a_end), (SUBLANES, LANES))
    cd_hi = cd.astype(BF16)
    cd_r = cd - cd_hi.astype(F32)
    cd_mid = cd_r.astype(BF16)
    cd_lo = (cd_r - cd_mid.astype(F32)).astype(BF16)
    cd_l = jnp.dot(jnp.concatenate([cd_hi, cd_mid, cd_lo], axis=1), expand, preferred_element_type=F32)[:1]

    xdt = xs * dt_l
    xdt_bf = xdt.astype(BF16)
    xdt_out_bf = (xdt * decay_out_l).astype(BF16)

    lane = lax.broadcasted_iota(jnp.int32, (L, LANES), 1)
    first_half = lane < P
    zero = jnp.zeros((L, LANES), BF16)

    ys = []
    for g in range(SSM_GROUPS):
        gs = slice(g * gw, (g + 1) * gw)
        cg = c_in[:, g * N:(g + 1) * N].astype(BF16)
        bg = b_in[:, g * N:(g + 1) * N]
        scores = lax.dot_general(cg, bg.astype(BF16), (((1,), (1,)), ((), ())), preferred_element_type=F32)
        state = state_ref[g]
        y_off = jnp.dot(cg, state.astype(BF16), preferred_element_type=F32) * decay_in_l[:, gs]
        upd = jnp.dot(bg.T.astype(BF16), xdt_out_bf[:, gs], preferred_element_type=F32)
        state_ref[g] = state * cd_l[:, gs] + upd
        for jp in range(heads_per_group // 2):
            ps = slice(g * gw + jp * LANES, g * gw + (jp + 1) * LANES)
            sd = []
            for h in (g * heads_per_group + 2 * jp, g * heads_per_group + 2 * jp + 1):
                decay = jnp.where(causal, jnp.exp(a_cs[:, h:h + 1] - a_cs_t[h:h + 1, :]), 0.0)
                sd.append((scores * decay).astype(BF16))
            xp = xdt_bf[:, ps]
            rhs = jnp.concatenate([jnp.where(first_half, xp, zero), jnp.where(first_half, zero, xp)], axis=0)
            y = jnp.dot(jnp.concatenate(sd, axis=1), rhs, preferred_element_type=F32)
            ys.append(y + y_off[:, jp * LANES:(jp + 1) * LANES] + dskip_ref[:, ps] * xs[:, ps])

    y = jnp.concatenate(ys, axis=1) * (z * _sigmoid(z))
    o_ref[0] = _rms(y, nw_ref[...]).astype(o_ref.dtype)


def ssd(zxd, cw, cb, dt_bias, a_log, d_skip, norm_w):
    b, t, width = zxd.shape
    L = SSM_CHUNK
    assert t % L == 0 and width == ZXD_WIDTH
    pad = lambda p: jnp.pad(p.astype(F32), (0, DT_PAD - SSM_HEADS)).reshape(1, DT_PAD)
    dskip_lanes = jnp.repeat(d_skip.astype(F32), SSM_HEAD_DIM).reshape(1, SSM_INNER)
    expand = (jnp.arange(LANES)[:, None] == jnp.arange(SSM_INNER)[None, :] // SSM_HEAD_DIM).astype(BF16)
    expand = jnp.concatenate([expand] * 3, axis=0)
    full = lambda shape: pl.BlockSpec(shape, lambda bi, c: (0,) * len(shape))
    return pl.pallas_call(
        _ssd_kernel,
        grid=(b, t // L),
        in_specs=[
            pl.BlockSpec((1, L, width), lambda bi, c: (bi, c, 0)),
            pl.BlockSpec((1, SUBLANES, width), lambda bi, c: (bi, jnp.maximum(c * (L // SUBLANES) - 1, 0), 0)),
            full((SSM_CONV, CONV_DIM)), full((1, CONV_DIM)), full((1, DT_PAD)), full((1, DT_PAD)),
            full((1, SSM_INNER)), full((1, SSM_INNER)), full((3 * LANES, SSM_INNER)),
        ],
        out_specs=pl.BlockSpec((1, L, SSM_INNER), lambda bi, c: (bi, c, 0)),
        out_shape=jax.ShapeDtypeStruct((b, t, SSM_INNER), BF16),
        scratch_shapes=[pltpu.VMEM((SSM_GROUPS, SSM_STATE, (SSM_HEADS // SSM_GROUPS) * SSM_HEAD_DIM), F32)],
        compiler_params=_params(("parallel", "arbitrary")),
        name="ssd",
    )(zxd, zxd, cw, cb.reshape(1, CONV_DIM), pad(dt_bias), pad(a_log), dskip_lanes, norm_w.reshape(1, SSM_INNER),
      expand)


def _dilated_kernel(q_ref, k_ref, v_ref, o_ref, xf, qd, kd, vd, bias, of, lf, *, span):
    si = pl.program_id(2)
    blk = DIL_BLOCK
    units = span // blk
    cur = si % 2
    prv = 1 - cur

    @pl.when(si == 0)
    def _():
        kd[:, 1] = jnp.zeros(kd.shape[:1] + kd.shape[2:], kd.dtype)
        vd[:, 1] = jnp.zeros(vd.shape[:1] + vd.shape[2:], vd.dtype)
        qpos = lax.broadcasted_iota(jnp.int32, (blk, 2 * blk), 0)
        kpos = lax.broadcasted_iota(jnp.int32, (blk, 2 * blk), 1)
        dist = qpos + blk - kpos
        for bi, (window, r) in enumerate(DIL_PATTERNS):
            band = jnp.logical_and(dist >= 0, dist <= window // r)
            bias[bi, 0] = jnp.where(band, 0.0, NEG_BIG)
            bias[bi, 1] = jnp.where(jnp.logical_and(band, kpos >= blk), 0.0, NEG_BIG)

    xf[0] = q_ref[0].astype(F32)
    xf[1] = k_ref[0].astype(F32)
    xf[2] = v_ref[0].astype(F32)
    for bi, (_, r) in enumerate(DIL_PATTERNS):
        rows = span // r
        if r == 1:
            kd[bi, cur] = k_ref[0]
            vd[bi, cur] = v_ref[0]
            continue
        for c in range(r):
            src = pl.ds(c, rows, stride=r)
            dst = pl.ds(c * rows, rows)
            qd[bi, dst, :] = xf[0, src, :].astype(BF16)
            kd[bi, cur, dst, :] = xf[1, src, :].astype(BF16)
            vd[bi, cur, dst, :] = xf[2, src, :].astype(BF16)

    first = jnp.where(si == 0, 1, 0)
    ones = jnp.ones((2 * blk, HEAD_DIM), BF16)
    for bi, (_, r) in enumerate(DIL_PATTERNS):
        rows = span // r
        for u in range(units):
            cls, nb = divmod(u, units // r)
            base = cls * rows + nb * blk
            here = pl.ds(base, blk)
            q = q_ref[0, here, :] if r == 1 else qd[bi, here, :]
            if nb > 0:
                before = pl.ds(base - blk, blk)
                k_prev, v_prev = kd[bi, cur, before, :], vd[bi, cur, before, :]
                b = bias[bi, 0]
            else:
                before = pl.ds(cls * rows + rows - blk, blk)
                k_prev, v_prev = kd[bi, prv, before, :], vd[bi, prv, before, :]
                b = bias[bi, first]
            k = jnp.concatenate([k_prev, kd[bi, cur, here, :]], axis=0)
            v = jnp.concatenate([v_prev, vd[bi, cur, here, :]], axis=0)
            s = lax.dot_general(q, k, (((1,), (1,)), ((), ())), preferred_element_type=F32) + b
            m = jnp.max(s, axis=-1, keepdims=True)
            p = jnp.exp2(s - m).astype(BF16)
            pv = jnp.dot(p, jnp.concatenate([v, ones], axis=1), preferred_element_type=F32)
            den = pv[:, HEAD_DIM:]
            o = pv[:, :HEAD_DIM] / den
            lse = m + jnp.log2(den)
            natural = pl.ds(base, blk) if r == 1 else pl.ds(cls + r * blk * nb, blk, stride=r)
            of[bi, natural, :] = o
            lf[bi, natural, :] = lse

    l0, l1, l2 = lf[0], lf[1], lf[2]
    m = jnp.maximum(jnp.maximum(l0, l1), l2)
    e0, e1, e2 = jnp.exp2(l0 - m), jnp.exp2(l1 - m), jnp.exp2(l2 - m)
    out = (e0 * of[0] + e1 * of[1] + e2 * of[2]) / (e0 + e1 + e2)
    o_ref[0] = out.astype(o_ref.dtype)


def dilated_attention(qkv):
    b, t, _ = qkv.shape
    span = DIL_BLOCK * max(r for _, r in DIL_PATTERNS)
    assert t % span == 0 and len(DIL_PATTERNS) == 3
    nh, nbr = DIL_HEADS, len(DIL_PATTERNS)
    spec = lambda off: pl.BlockSpec((1, span, HEAD_DIM), lambda bi, h, s: (bi, s, off + h))
    return pl.pallas_call(
        functools.partial(_dilated_kernel, span=span),
        grid=(b, nh, t // span),
        in_specs=[spec(0), spec(nh), spec(2 * nh)],
        out_specs=pl.BlockSpec((1, span, HEAD_DIM), lambda bi, h, s: (bi, s, h)),
        out_shape=jax.ShapeDtypeStruct((b, t, nh * HEAD_DIM), BF16),
        scratch_shapes=[
            pltpu.VMEM((3, span, HEAD_DIM), F32),
            pltpu.VMEM((nbr, span, HEAD_DIM), BF16),
            pltpu.VMEM((nbr, 2, span, HEAD_DIM), BF16),
            pltpu.VMEM((nbr, 2, span, HEAD_DIM), BF16),
            pltpu.VMEM((nbr, 2, DIL_BLOCK, 2 * DIL_BLOCK), F32),
            pltpu.VMEM((nbr, span, HEAD_DIM), F32),
            pltpu.VMEM((nbr, span, HEAD_DIM), F32),
        ],
        compiler_params=_params(("parallel", "parallel", "arbitrary")),
        name="dilated_attention",
    )(qkv, qkv, qkv)


def _rmsnorm_kernel(x_ref, w_ref, o_ref):
    o_ref[...] = _rms(x_ref[...], w_ref[...])


def rmsnorm(x, w, tm=512):
    m, d = x.shape
    tm = min(tm, m)
    assert m % tm == 0
    return pl.pallas_call(
        _rmsnorm_kernel,
        grid=(m // tm,),
        in_specs=[pl.BlockSpec((tm, d), lambda i: (i, 0)), pl.BlockSpec((1, d), lambda i: (0, 0))],
        out_specs=pl.BlockSpec((tm, d), lambda i: (i, 0)),
        out_shape=jax.ShapeDtypeStruct((m, d), F32),
        compiler_params=_params(("parallel",)),
        name="rmsnorm",
    )(x, w.reshape(1, d))


def _q_colscale(n_cols, q_cols):
    return jnp.where(jnp.arange(n_cols) < q_cols, LOG2_E * HEAD_DIM ** -0.5, 1.0).astype(F32)


def _even_mixer(h, bsz, seq, i, nw, ffn_nw, w_qkv, w_zxd, conv_w, conv_b, dt_bias, a_log, d_skip, ssm_norm_w, w_out):
    qkv = norm_matmul(h, nw, w_qkv, i, BF16)
    zxd = norm_matmul(h, nw, w_zxd, i, F32, tn=ZXD_WIDTH // 3)
    o_a = sb_attention(qkv.reshape(bsz, seq, 3 * SB_WIDTH)).reshape(bsz * seq, SB_WIDTH)
    o_b = ssd(zxd.reshape(bsz, seq, ZXD_WIDTH), conv_w, conv_b, dt_bias, a_log, d_skip,
              ssm_norm_w).reshape(bsz * seq, SSM_INNER)
    return matmul_residual([o_a, o_b], w_out, i, h, ffn_nw)


def _odd_mixer(h, bsz, seq, i, nw, ffn_nw, w_qkv, w_out):
    qkv = norm_matmul(h, nw, w_qkv, i, BF16)
    o = dilated_attention(qkv.reshape(bsz, seq, 3 * D_MODEL)).reshape(bsz * seq, D_MODEL)
    return matmul_residual([o], w_out, i, h, ffn_nw)


def kernel(x, mix_norm_w, ffn_norm_w, final_norm_w, ev_w_in, ev_conv_w, ev_conv_b, ev_dt_bias, ev_a_log, ev_d_skip, ev_ssm_norm_w, ev_w_out, od_w_in, od_w_out, ffn_w_gate, ffn_w_up, ffn_conv_w, ffn_conv_b, ffn_w_down):
    bsz, seq, d = x.shape
    ev_qkv, ev_zxd = split_in_proj(ev_w_in, _q_colscale(ev_w_in.shape[-1], SB_WIDTH))
    od_qkv = cast_bf16(od_w_in, _q_colscale(od_w_in.shape[-1], D_MODEL))
    ev_out, od_out = cast_bf16(ev_w_out), cast_bf16(od_w_out)
    w_gate, w_up, w_down = cast_bf16(ffn_w_gate), cast_bf16(ffn_w_up), cast_bf16(ffn_w_down)
    h = x.reshape(bsz * seq, d)
    for layer in range(DEPTH):
        i = layer // 2
        if layer % 2 == 0:
            h, xn = _even_mixer(h, bsz, seq, i, mix_norm_w[layer], ffn_norm_w[layer], ev_qkv, ev_zxd,
                                ev_conv_w[i], ev_conv_b[i], ev_dt_bias[i], ev_a_log[i], ev_d_skip[i],
                                ev_ssm_norm_w[i], ev_out)
        else:
            h, xn = _odd_mixer(h, bsz, seq, i, mix_norm_w[layer], ffn_norm_w[layer], od_qkv, od_out)
        h = conv_ffn(xn, h, seq, w_gate, w_up, ffn_conv_w, ffn_conv_b, w_down, layer)
    return rmsnorm(h, final_norm_w).reshape(bsz, seq, d)
```

```python
import functools

import jax
import jax.numpy as jnp
from jax import lax
from jax.experimental import pallas as pl
from jax.experimental.pallas import tpu as pltpu

F32 = jnp.float32
BF16 = jnp.bfloat16

D_MODEL = 2048
DEPTH = 4
SB_HEADS = 8
HEAD_DIM = 128
SB_WIDTH = SB_HEADS * HEAD_DIM
SSM_HEAD_DIM = 64
SSM_INNER = 1024
SSM_HEADS = 16
SSM_GROUPS = 2
SSM_STATE = 128
SSM_CONV = 4
SSM_CHUNK = 128
CONV_DIM = SSM_INNER + 2 * SSM_GROUPS * SSM_STATE
DIL_HEADS = 16
DIL_PATTERNS = ((128, 1), (512, 4), (2048, 16))
DIL_BLOCK = 128
D_FF = 5632
FFN_CONV = 3
EPS = 1e-6

LANES = 128
SUBLANES = 8
VMEM_LIMIT = 56 * 1024 * 1024
DT_PAD = LANES
ZXD_WIDTH = SSM_INNER + CONV_DIM + DT_PAD

SB_EXIT_LOG2 = 160.0
LOG2_E = 1.4426950408889634
NEG_BIG = -1e30


def _params(sem, vmem=VMEM_LIMIT):
    return pltpu.CompilerParams(dimension_semantics=sem, vmem_limit_bytes=vmem)


def _rms(x, w):
    ms = jnp.mean(x * x, axis=-1, keepdims=True)
    return x * lax.rsqrt(ms + EPS) * w


def _sigmoid(x):
    return 1.0 / (1.0 + jnp.exp(-x))


def _shift_rows(x, halo, s):
    n = x.shape[0]
    r = pltpu.roll(x, s, axis=0)
    hr = pltpu.roll(halo, s, axis=0)
    rid = lax.broadcasted_iota(jnp.int32, hr.shape, 0)
    top = jnp.where(rid < s, hr, r[:SUBLANES])
    return jnp.concatenate([top, r[SUBLANES:]], axis=0) if n > SUBLANES else top


CAST_BLOCK_BYTES = 6 * 1024 * 1024
BF16_ROWS = 2 * SUBLANES


def _block_rows(total_rows, row_bytes):
    best = None
    for rows in range(BF16_ROWS, total_rows + 1, BF16_ROWS):
        if total_rows % rows == 0 and rows * row_bytes <= CAST_BLOCK_BYTES:
            best = rows
    assert best is not None
    return best


def _split_in_proj_kernel(w_ref, s_ref, qkv_ref, zxd_ref, *, qkv_cols, zx_cols, dt_cols):
    w = w_ref[...] * s_ref[...]
    qkv_ref[...] = w[:, :qkv_cols].astype(BF16)
    zxd_ref[:, :zx_cols] = w[:, qkv_cols:qkv_cols + zx_cols].astype(BF16)
    zxd_ref[:, zx_cols:] = jnp.zeros((w.shape[0], DT_PAD), BF16)
    zxd_ref[:, zx_cols:zx_cols + dt_cols] = w[:, qkv_cols + zx_cols:].astype(BF16)


def split_in_proj(w_in, colscale):
    s, d, n = w_in.shape
    qkv_cols = 3 * SB_WIDTH
    zx_cols = SSM_INNER + CONV_DIM
    assert n == qkv_cols + zx_cols + SSM_HEADS
    rows = _block_rows(d, n * 4)
    block = lambda width: pl.BlockSpec((None, rows, width), lambda si, i: (si, i, 0))
    return pl.pallas_call(
        functools.partial(_split_in_proj_kernel, qkv_cols=qkv_cols, zx_cols=zx_cols, dt_cols=SSM_HEADS),
        grid=(s, d // rows),
        in_specs=[block(n), pl.BlockSpec((1, n), lambda si, i: (0, 0))],
        out_specs=[block(qkv_cols), block(ZXD_WIDTH)],
        out_shape=[jax.ShapeDtypeStruct((s, d, qkv_cols), BF16), jax.ShapeDtypeStruct((s, d, ZXD_WIDTH), BF16)],
        compiler_params=_params(("parallel", "parallel")),
        name="split_in_proj",
    )(w_in, colscale.reshape(1, n))


def _norm_matmul_kernel(x_ref, nw_ref, w_ref, o_ref, xn_ref):
    @pl.when(pl.program_id(1) == 0)
    def _():
        xn_ref[...] = _rms(x_ref[...], nw_ref[...]).astype(BF16)

    o_ref[...] = jnp.dot(xn_ref[...], w_ref[...], preferred_element_type=F32).astype(o_ref.dtype)


def norm_matmul(x, nw, w, layer, out_dtype, tm=1024, tn=1536):
    m, d = x.shape
    n = w.shape[2]
    tm, tn = min(tm, m), min(tn, n)
    assert m % tm == 0 and n % tn == 0
    return pl.pallas_call(
        _norm_matmul_kernel,
        grid=(m // tm, n // tn),
        in_specs=[
            pl.BlockSpec((tm, d), lambda i, j: (i, 0)),
            pl.BlockSpec((1, d), lambda i, j: (0, 0)),
            pl.BlockSpec((None, d, tn), lambda i, j: (layer, 0, j)),
        ],
        out_specs=pl.BlockSpec((tm, tn), lambda i, j: (i, j)),
        out_shape=jax.ShapeDtypeStruct((m, n), out_dtype),
        scratch_shapes=[pltpu.VMEM((tm, d), BF16)],
        compiler_params=_params(("parallel", "arbitrary")),
        name="norm_matmul",
    )(x, nw.reshape(1, d), w)


def _matmul_residual_kernel(*refs, n_in):
    xs, ws = refs[:n_in], refs[n_in:2 * n_in]
    h_ref, nw_ref, o_ref, xn_ref = refs[2 * n_in:]
    acc = h_ref[...]
    for x_ref, w_ref in zip(xs, ws):
        acc = acc + jnp.dot(x_ref[...], w_ref[...], preferred_element_type=F32)
    o_ref[...] = acc
    xn_ref[...] = _rms(acc, nw_ref[...]).astype(BF16)


def matmul_residual(xs, w, layer, h, nw, tm=512):
    m, n = h.shape
    tm = min(tm, m)
    kx = xs[0].shape[1]
    assert m % tm == 0 and all(x.shape[1] == kx for x in xs) and kx * len(xs) == w.shape[1]
    in_specs = [pl.BlockSpec((tm, kx), lambda i: (i, 0)) for _ in xs]
    in_specs += [pl.BlockSpec((None, kx, n), lambda i, p=p: (layer, p, 0)) for p in range(len(xs))]
    in_specs += [pl.BlockSpec((tm, n), lambda i: (i, 0)), pl.BlockSpec((1, n), lambda i: (0, 0))]
    return pl.pallas_call(
        functools.partial(_matmul_residual_kernel, n_in=len(xs)),
        grid=(m // tm,),
        in_specs=in_specs,
        out_specs=[pl.BlockSpec((tm, n), lambda i: (i, 0)), pl.BlockSpec((tm, n), lambda i: (i, 0))],
        out_shape=[jax.ShapeDtypeStruct((m, n), F32), jax.ShapeDtypeStruct((m, n), BF16)],
        compiler_params=_params(("parallel",)),
        name="matmul_residual",
    )(*xs, *([w] * len(xs)), h, nw.reshape(1, n))


def _conv_ffn_kernel(xn_ref, h_ref, wg_ref, wu_ref, cw_ref, cb_ref, wd_ref, o_ref, act_ref, carry_ref,
                     *, n_f, tf, tiles_per_seq):
    i, j = pl.program_id(0), pl.program_id(1)

    @pl.when(j < n_f)
    def _():
        @pl.when(i % tiles_per_seq == 0)
        def _():
            carry_ref[j] = jnp.zeros(carry_ref.shape[1:], F32)

        xn = xn_ref[...]
        g = jnp.dot(xn, wg_ref[...], preferred_element_type=F32)
        u = jnp.dot(xn, wu_ref[...], preferred_element_type=F32)
        halo = carry_ref[j]
        carry_ref[j] = g[-SUBLANES:]
        cw = cw_ref[...]
        c = cw[2:3] * g + cw[1:2] * _shift_rows(g, halo, 1) + cw[0:1] * _shift_rows(g, halo, 2) + cb_ref[...]
        act_ref[j] = (c * _sigmoid(c) * u).astype(BF16)

    @pl.when(j >= n_f)
    def _():
        acc = h_ref[...]
        for f in range(n_f):
            acc = acc + jnp.dot(act_ref[f], wd_ref[f * tf:(f + 1) * tf, :], preferred_element_type=F32)
        o_ref[...] = acc


def conv_ffn(xn, h, seq_len, wg, wu, cw, cb, wd, layer, tm=1024, tf=512, tn=512):
    m, d = h.shape
    dff = wg.shape[2]
    tm = min(tm, seq_len)
    assert seq_len % tm == 0 and dff % tf == 0 and m % seq_len == 0 and d % tn == 0
    n_f = dff // tf
    gate = lambda i, j: (layer, 0, jnp.minimum(j, n_f - 1))
    down = lambda i, j: (layer, 0, jnp.maximum(j - n_f, 0))
    out = lambda i, j: (i, jnp.maximum(j - n_f, 0))
    return pl.pallas_call(
        functools.partial(_conv_ffn_kernel, n_f=n_f, tf=tf, tiles_per_seq=seq_len // tm),
        grid=(m // tm, n_f + d // tn),
        in_specs=[
            pl.BlockSpec((tm, d), lambda i, j: (i, 0)),
            pl.BlockSpec((tm, tn), out),
            pl.BlockSpec((None, d, tf), gate),
            pl.BlockSpec((None, d, tf), gate),
            pl.BlockSpec((None, FFN_CONV, tf), gate),
            pl.BlockSpec((None, 1, tf), gate),
            pl.BlockSpec((None, dff, tn), down),
        ],
        out_specs=pl.BlockSpec((tm, tn), out),
        out_shape=jax.ShapeDtypeStruct((m, d), F32),
        scratch_shapes=[pltpu.VMEM((n_f, tm, tf), BF16), pltpu.VMEM((n_f, SUBLANES, tf), F32)],
        compiler_params=_params(("arbitrary", "arbitrary")),
        name="conv_ffn",
    )(xn, h, wg, wu, cw, cb.reshape(cb.shape[0], 1, dff), wd)


def _sb_kernel(*refs, tile, heads, scaled):
    q_ref, k_ref, v_ref = refs[:3]
    n_in = 3 + len(scaled) + sum(scaled)
    o_ref = refs[n_in]
    pos = 3
    for has_scale, wo_ref in zip(scaled, refs[n_in + 1:]):
        w = refs[pos][...]
        if has_scale:
            w = w * refs[pos + 1][...]
        wo_ref[...] = w.astype(BF16)
        pos += 2 if has_scale else 1
    qi = pl.program_id(2)
    row = lax.broadcasted_iota(jnp.int32, (tile, tile), 0)
    col = lax.broadcasted_iota(jnp.int32, (tile, tile), 1)
    later = (row > col).astype(BF16)
    later2 = jnp.concatenate([later, later], axis=0)
    strictly_before = col < row

    def logits(q, hs, j, diagonal):
        k = k_ref[0, pl.ds(pl.multiple_of(j * tile, tile), tile), hs]
        z = lax.dot_general(q, k, (((1,), (1,)), ((), ())), preferred_element_type=F32)
        sp = jnp.maximum(z, 0.0) + jnp.log2(1.0 + jnp.exp2(-jnp.abs(z)))
        log_keep = -sp
        if diagonal:
            log_keep = jnp.where(strictly_before, log_keep, 0.0)
        hi = log_keep.astype(BF16)
        lo = (log_keep - hi.astype(F32)).astype(BF16)
        suffix = jnp.dot(jnp.concatenate([hi, lo], axis=1), later2, preferred_element_type=F32)
        return (z - sp) + suffix, suffix[:, :1] + log_keep[:, :1]

    def weighted(hs, j, w):
        v = v_ref[0, pl.ds(pl.multiple_of(j * tile, tile), tile), hs]
        return jnp.dot(w.astype(BF16), v, preferred_element_type=F32)

    jb = jnp.maximum(qi - 1, 0)
    no_left_tile = jnp.where(qi > 0, 0.0, NEG_BIG)
    head_slices = [slice(hd * HEAD_DIM, (hd + 1) * HEAD_DIM) for hd in range(heads)]
    first_two = []
    for hs in head_slices:
        q = q_ref[0, :, hs]
        la, ta = logits(q, hs, qi, True)
        lb, tb = logits(q, hs, jb, False)
        acc = weighted(hs, qi, jnp.where(strictly_before, jnp.exp2(la), 0.0))
        acc = acc + weighted(hs, jb, jnp.exp2(lb + (ta + no_left_tile)))
        first_two.append((q, ta + tb, acc))

    for hs, (q, run, acc) in zip(head_slices, first_two):

        def cond(carry):
            j, alive, _, _ = carry
            return jnp.logical_and(j >= 0, alive > 0)

        def body(carry, q=q, hs=hs):
            j, _, run, acc = carry
            lj, tj = logits(q, hs, j, False)
            acc = acc + weighted(hs, j, jnp.exp2(lj + run))
            run = run + tj
            alive = (jnp.max(run) > -SB_EXIT_LOG2).astype(jnp.int32)
            return j - 1, alive, run, acc

        alive0 = (jnp.max(run) > -SB_EXIT_LOG2).astype(jnp.int32)
        _, _, _, acc = lax.while_loop(cond, body, (qi - 2, alive0, run, acc))
        o_ref[0, :, hs] = acc.astype(o_ref.dtype)


def sb_attention(qkv, cast=(), tile=256, heads=2):
    b, t, _ = qkv.shape
    tile = min(tile, t)
    assert t % tile == 0 and SB_HEADS % heads == 0
    groups = SB_HEADS // heads
    width = heads * HEAD_DIM
    nq = t // tile
    steps = b * groups * nq
    in_specs = [
        pl.BlockSpec((1, tile, width), lambda bi, g, i: (bi, i, g)),
        pl.BlockSpec((1, t, width), lambda bi, g, i: (bi, 0, groups + g)),
        pl.BlockSpec((1, t, width), lambda bi, g, i: (bi, 0, 2 * groups + g)),
    ]
    out_specs = [pl.BlockSpec((1, tile, width), lambda bi, g, i: (bi, i, g))]
    out_shape = [jax.ShapeDtypeStruct((b, t, SB_WIDTH), BF16)]
    args = [qkv, qkv, qkv]
    for w, colscale in cast:
        w = w.reshape(-1, w.shape[-1])
        rows_total, n = w.shape
        share = 1
        while rows_total % (steps // share) or (rows_total // (steps // share)) % BF16_ROWS:
            share *= 2
            assert share <= steps
        rows = rows_total // (steps // share)
        spec = pl.BlockSpec((rows, n), lambda bi, g, i, share=share: (((bi * groups + g) * nq + i) // share, 0))
        in_specs.append(spec)
        args.append(w)
        if colscale is not None:
            in_specs.append(pl.BlockSpec((1, n), lambda bi, g, i: (0, 0)))
            args.append(colscale.reshape(1, n))
        out_specs.append(spec)
        out_shape.append(jax.ShapeDtypeStruct((rows_total, n), BF16))
    outs = pl.pallas_call(
        functools.partial(_sb_kernel, tile=tile, heads=heads, scaled=tuple(s is not None for _, s in cast)),
        grid=(b, groups, nq),
        in_specs=in_specs,
        out_specs=out_specs,
        out_shape=out_shape,
        compiler_params=_params(("parallel", "parallel", "arbitrary")),
        name="sb_attention",
    )(*args)
    return (outs[0], *[o.reshape(w.shape) for o, (w, _) in zip(outs[1:], cast)])


def _ssd_kernel(zxd_ref, halo_ref, cw_ref, cb_ref, dtb_ref, alog_ref, dskip_ref, nw_ref, expand_ref, o_ref, state_ref):
    c = pl.program_id(1)
    L, N, P = SSM_CHUNK, SSM_STATE, SSM_HEAD_DIM
    heads_per_group = SSM_HEADS // SSM_GROUPS
    gw = heads_per_group * P

    @pl.when(c == 0)
    def _():
        state_ref[...] = jnp.zeros_like(state_ref)

    blk = zxd_ref[0]
    z = blk[:, :SSM_INNER]
    xbc = blk[:, SSM_INNER:SSM_INNER + CONV_DIM]
    dt_raw = blk[:, SSM_INNER + CONV_DIM:]
    halo = jnp.where(c == 0, 0.0, halo_ref[0][:, SSM_INNER:SSM_INNER + CONV_DIM])

    cw = cw_ref[...]
    conv = cw[3:4] * xbc + cb_ref[...]
    for s in range(1, SSM_CONV):
        conv = conv + cw[3 - s:4 - s] * _shift_rows(xbc, halo, s)
    xbc = conv * _sigmoid(conv)
    xs = xbc[:, :SSM_INNER]
    b_in = xbc[:, SSM_INNER:SSM_INNER + SSM_GROUPS * N]
    c_in = xbc[:, SSM_INNER + SSM_GROUPS * N:]

    pre = dt_raw + dtb_ref[...]
    dt = jnp.maximum(pre, 0.0) + jnp.log1p(jnp.exp(-jnp.abs(pre)))
    a = -jnp.exp(alog_ref[...])
    row = lax.broadcasted_iota(jnp.int32, (L, L), 0)
    col = lax.broadcasted_iota(jnp.int32, (L, L), 1)
    causal = row >= col
    tri = causal.astype(F32)
    a_cs = jnp.dot(tri, dt * a, preferred_element_type=F32, precision=lax.Precision.HIGHEST)
    a_cs_t = a_cs.T
    a_end = a_cs[L - 1:L, :]

    expand = expand_ref[...]
    spread = lambda v: jnp.dot(v.astype(BF16), expand[:LANES], preferred_element_type=F32)
    dt_l = spread(dt)
    decay_in_l = spread(jnp.exp(a_cs))
    decay_out_l = spread(jnp.exp(a_end - a_cs))
    cd = jnp.broadcast_to(jnp.exp(a_end), (SUBLANES, LANES))
    cd_hi = cd.astype(BF16)
    cd_r = cd - cd_hi.astype(F32)
    cd_mid = cd_r.astype(BF16)
    cd_lo = (cd_r - cd_mid.astype(F32)).astype(BF16)
    cd_l = jnp.dot(jnp.concatenate([cd_hi, cd_mid, cd_lo], axis=1), expand, preferred_element_type=F32)[:1]

    xdt = xs * dt_l
    xdt_bf = xdt.astype(BF16)
    xdt_out_bf = (xdt * decay_out_l).astype(BF16)

    lane = lax.broadcasted_iota(jnp.int32, (L, LANES), 1)
    first_half = lane < P
    zero = jnp.zeros((L, LANES), BF16)

    ys = []
    for g in range(SSM_GROUPS):
        gs = slice(g * gw, (g + 1) * gw)
        cg = c_in[:, g * N:(g + 1) * N].astype(BF16)
        bg = b_in[:, g * N:(g + 1) * N]
        scores = lax.dot_general(cg, bg.astype(BF16), (((1,), (1,)), ((), ())), preferred_element_type=F32)
        state = state_ref[g]
        y_off = jnp.dot(cg, state.astype(BF16), preferred_element_type=F32) * decay_in_l[:, gs]
        upd = jnp.dot(bg.T.astype(BF16), xdt_out_bf[:, gs], preferred_element_type=F32)
        state_ref[g] = state * cd_l[:, gs] + upd
        for jp in range(heads_per_group // 2):
            ps = slice(g * gw + jp * LANES, g * gw + (jp + 1) * LANES)
            sd = []
            for h in (g * heads_per_group + 2 * jp, g * heads_per_group + 2 * jp + 1):
                decay = jnp.where(causal, jnp.exp(a_cs[:, h:h + 1] - a_cs_t[h:h + 1, :]), 0.0)
                sd.append((scores * decay).astype(BF16))
            xp = xdt_bf[:, ps]
            rhs = jnp.concatenate([jnp.where(first_half, xp, zero), jnp.where(first_half, zero, xp)], axis=0)
            y = jnp.dot(jnp.concatenate(sd, axis=1), rhs, preferred_element_type=F32)
            ys.append(y + y_off[:, jp * LANES:(jp + 1) * LANES] + dskip_ref[:, ps] * xs[:, ps])

    y = jnp.concatenate(ys, axis=1) * (z * _sigmoid(z))
    o_ref[0] = _rms(y, nw_ref[...]).astype(o_ref.dtype)


def ssd(zxd, cw, cb, dt_bias, a_log, d_skip, norm_w):
    b, t, width = zxd.shape
    L = SSM_CHUNK
    assert t % L == 0 and width == ZXD_WIDTH
    pad = lambda p: jnp.pad(p.astype(F32), (0, DT_PAD - SSM_HEADS)).reshape(1, DT_PAD)
    dskip_lanes = jnp.repeat(d_skip.astype(F32), SSM_HEAD_DIM).reshape(1, SSM_INNER)
    expand = (jnp.arange(LANES)[:, None] == jnp.arange(SSM_INNER)[None, :] // SSM_HEAD_DIM).astype(BF16)
    expand = jnp.concatenate([expand] * 3, axis=0)
    full = lambda shape: pl.BlockSpec(shape, lambda bi, c: (0,) * len(shape))
    return pl.pallas_call(
        _ssd_kernel,
        grid=(b, t // L),
        in_specs=[
            pl.BlockSpec((1, L, width), lambda bi, c: (bi, c, 0)),
            pl.BlockSpec((1, SUBLANES, width), lambda bi, c: (bi, jnp.maximum(c * (L // SUBLANES) - 1, 0), 0)),
            full((SSM_CONV, CONV_DIM)), full((1, CONV_DIM)), full((1, DT_PAD)), full((1, DT_PAD)),
            full((1, SSM_INNER)), full((1, SSM_INNER)), full((3 * LANES, SSM_INNER)),
        ],
        out_specs=pl.BlockSpec((1, L, SSM_INNER), lambda bi, c: (bi, c, 0)),
        out_shape=jax.ShapeDtypeStruct((b, t, SSM_INNER), BF16),
        scratch_shapes=[pltpu.VMEM((SSM_GROUPS, SSM_STATE, (SSM_HEADS // SSM_GROUPS) * SSM_HEAD_DIM), F32)],
        compiler_params=_params(("parallel", "arbitrary")),
        name="ssd",
    )(zxd, zxd, cw, cb.reshape(1, CONV_DIM), pad(dt_bias), pad(a_log), dskip_lanes, norm_w.reshape(1, SSM_INNER),
      expand)


def _dilated_kernel(q_ref, k_ref, v_ref, o_ref, xf, qd, kd, vd, bias, of, lf, *, span):
    si = pl.program_id(2)
    blk = DIL_BLOCK
    units = span // blk
    cur = si % 2
    prv = 1 - cur

    @pl.when(si == 0)
    def _():
        kd[:, 1] = jnp.zeros(kd.shape[:1] + kd.shape[2:], kd.dtype)
        vd[:, 1] = jnp.zeros(vd.shape[:1] + vd.shape[2:], vd.dtype)
        qpos = lax.broadcasted_iota(jnp.int32, (blk, 2 * blk), 0)
        kpos = lax.broadcasted_iota(jnp.int32, (blk, 2 * blk), 1)
        dist = qpos + blk - kpos
        for bi, (window, r) in enumerate(DIL_PATTERNS):
            band = jnp.logical_and(dist >= 0, dist <= window // r)
            bias[bi, 0] = jnp.where(band, 0.0, NEG_BIG)
            bias[bi, 1] = jnp.where(jnp.logical_and(band, kpos >= blk), 0.0, NEG_BIG)

    xf[0] = q_ref[0].astype(F32)
    xf[1] = k_ref[0].astype(F32)
    xf[2] = v_ref[0].astype(F32)
    for bi, (_, r) in enumerate(DIL_PATTERNS):
        rows = span // r
        if r == 1:
            kd[bi, cur] = k_ref[0]
            vd[bi, cur] = v_ref[0]
            continue
        for c in range(r):
            src = pl.ds(c, rows, stride=r)
            dst = pl.ds(c * rows, rows)
            qd[bi, dst, :] = xf[0, src, :].astype(BF16)
            kd[bi, cur, dst, :] = xf[1, src, :].astype(BF16)
            vd[bi, cur, dst, :] = xf[2, src, :].astype(BF16)

    first = jnp.where(si == 0, 1, 0)
    ones = jnp.ones((2 * blk, HEAD_DIM), BF16)
    for bi, (_, r) in enumerate(DIL_PATTERNS):
        rows = span // r
        for u in range(units):
            cls, nb = divmod(u, units // r)
            base = cls * rows + nb * blk
            here = pl.ds(base, blk)
            q = q_ref[0, here, :] if r == 1 else qd[bi, here, :]
            if nb > 0:
                before = pl.ds(base - blk, blk)
                k_prev, v_prev = kd[bi, cur, before, :], vd[bi, cur, before, :]
                b = bias[bi, 0]
            else:
                before = pl.ds(cls * rows + rows - blk, blk)
                k_prev, v_prev = kd[bi, prv, before, :], vd[bi, prv, before, :]
                b = bias[bi, first]
            k = jnp.concatenate([k_prev, kd[bi, cur, here, :]], axis=0)
            v = jnp.concatenate([v_prev, vd[bi, cur, here, :]], axis=0)
            s = lax.dot_general(q, k, (((1,), (1,)), ((), ())), preferred_element_type=F32) + b
            m = jnp.max(s, axis=-1, keepdims=True)
            p = jnp.exp2(s - m).astype(BF16)
            pv = jnp.dot(p, jnp.concatenate([v, ones], axis=1), preferred_element_type=F32)
            den = pv[:, HEAD_DIM:]
            o = pv[:, :HEAD_DIM] / den
            lse = m + jnp.log2(den)
            natural = pl.ds(base, blk) if r == 1 else pl.ds(cls + r * blk * nb, blk, stride=r)
            of[bi, natural, :] = o
            lf[bi, natural, :] = lse

    l0, l1, l2 = lf[0], lf[1], lf[2]
    m = jnp.maximum(jnp.maximum(l0, l1), l2)
    e0, e1, e2 = jnp.exp2(l0 - m), jnp.exp2(l1 - m), jnp.exp2(l2 - m)
    out = (e0 * of[0] + e1 * of[1] + e2 * of[2]) / (e0 + e1 + e2)
    o_ref[0] = out.astype(o_ref.dtype)


def dilated_attention(qkv):
    b, t, _ = qkv.shape
    span = DIL_BLOCK * max(r for _, r in DIL_PATTERNS)
    assert t % span == 0 and len(DIL_PATTERNS) == 3
    nh, nbr = DIL_HEADS, len(DIL_PATTERNS)
    spec = lambda off: pl.BlockSpec((1, span, HEAD_DIM), lambda bi, h, s: (bi, s, off + h))
    return pl.pallas_call(
        functools.partial(_dilated_kernel, span=span),
        grid=(b, nh, t // span),
        in_specs=[spec(0), spec(nh), spec(2 * nh)],
        out_specs=pl.BlockSpec((1, span, HEAD_DIM), lambda bi, h, s: (bi, s, h)),
        out_shape=jax.ShapeDtypeStruct((b, t, nh * HEAD_DIM), BF16),
        scratch_shapes=[
            pltpu.VMEM((3, span, HEAD_DIM), F32),
            pltpu.VMEM((nbr, span, HEAD_DIM), BF16),
            pltpu.VMEM((nbr, 2, span, HEAD_DIM), BF16),
            pltpu.VMEM((nbr, 2, span, HEAD_DIM), BF16),
            pltpu.VMEM((nbr, 2, DIL_BLOCK, 2 * DIL_BLOCK), F32),
            pltpu.VMEM((nbr, span, HEAD_DIM), F32),
            pltpu.VMEM((nbr, span, HEAD_DIM), F32),
        ],
        compiler_params=_params(("parallel", "parallel", "arbitrary")),
        name="dilated_attention",
    )(qkv, qkv, qkv)


def _rmsnorm_kernel(x_ref, w_ref, o_ref):
    o_ref[...] = _rms(x_ref[...], w_ref[...])


def rmsnorm(x, w, tm=512):
    m, d = x.shape
    tm = min(tm, m)
    assert m % tm == 0
    return pl.pallas_call(
        _rmsnorm_kernel,
        grid=(m // tm,),
        in_specs=[pl.BlockSpec((tm, d), lambda i: (i, 0)), pl.BlockSpec((1, d), lambda i: (0, 0))],
        out_specs=pl.BlockSpec((tm, d), lambda i: (i, 0)),
        out_shape=jax.ShapeDtypeStruct((m, d), F32),
        compiler_params=_params(("parallel",)),
        name="rmsnorm",
    )(x, w.reshape(1, d))


def _q_colscale(n_cols, q_cols):
    return jnp.where(jnp.arange(n_cols) < q_cols, LOG2_E * HEAD_DIM ** -0.5, 1.0).astype(F32)


def _even_mixer(h, bsz, seq, i, nw, ffn_nw, w_qkv, w_zxd, conv_w, conv_b, dt_bias, a_log, d_skip, ssm_norm_w, w_out,
                cast=()):
    qkv = norm_matmul(h, nw, w_qkv, i, BF16)
    zxd = norm_matmul(h, nw, w_zxd, i, F32, tn=ZXD_WIDTH // 3)
    o_a, *cast_bf16s = sb_attention(qkv.reshape(bsz, seq, 3 * SB_WIDTH), cast)
    o_b = ssd(zxd.reshape(bsz, seq, ZXD_WIDTH), conv_w, conv_b, dt_bias, a_log, d_skip,
              ssm_norm_w).reshape(bsz * seq, SSM_INNER)
    if w_out is None:
        w_out = cast_bf16s[0]
    h, xn = matmul_residual([o_a.reshape(bsz * seq, SB_WIDTH), o_b], w_out, i, h, ffn_nw)
    return h, xn, cast_bf16s


def _odd_mixer(h, bsz, seq, i, nw, ffn_nw, w_qkv, w_out):
    qkv = norm_matmul(h, nw, w_qkv, i, BF16)
    o = dilated_attention(qkv.reshape(bsz, seq, 3 * D_MODEL)).reshape(bsz * seq, D_MODEL)
    return matmul_residual([o], w_out, i, h, ffn_nw)


def kernel(x, mix_norm_w, ffn_norm_w, final_norm_w, ev_w_in, ev_conv_w, ev_conv_b, ev_dt_bias, ev_a_log, ev_d_skip, ev_ssm_norm_w, ev_w_out, od_w_in, od_w_out, ffn_w_gate, ffn_w_up, ffn_conv_w, ffn_conv_b, ffn_w_down):
    bsz, seq, d = x.shape
    ev_qkv, ev_zxd = split_in_proj(ev_w_in, _q_colscale(ev_w_in.shape[-1], SB_WIDTH))
    later_casts = ((ev_w_out, None), (od_w_in, _q_colscale(od_w_in.shape[-1], D_MODEL)), (od_w_out, None),
                   (ffn_w_gate, None), (ffn_w_up, None), (ffn_w_down, None))
    h = x.reshape(bsz * seq, d)
    for layer in range(DEPTH):
        i = layer // 2
        if layer % 2 == 0:
            h, xn, converted = _even_mixer(h, bsz, seq, i, mix_norm_w[layer], ffn_norm_w[layer], ev_qkv, ev_zxd,
                                           ev_conv_w[i], ev_conv_b[i], ev_dt_bias[i], ev_a_log[i], ev_d_skip[i],
                                           ev_ssm_norm_w[i], None if layer == 0 else ev_out,
                                           cast=later_casts if layer == 0 else ())
            if layer == 0:
                ev_out, od_qkv, od_out, w_gate, w_up, w_down = converted
        else:
            h, xn = _odd_mixer(h, bsz, seq, i, mix_norm_w[layer], ffn_norm_w[layer], od_qkv, od_out)
        h = conv_ffn(xn, h, seq, w_gate, w_up, ffn_conv_w, ffn_conv_b, w_down, layer)
    return rmsnorm(h, final_norm_w).reshape(bsz, seq, d)
```

```python
import functools

import jax
import jax.numpy as jnp
from jax import lax
from jax.experimental import pallas as pl
from jax.experimental.pallas import tpu as pltpu

F32 = jnp.float32
BF16 = jnp.bfloat16

D_MODEL = 2048
DEPTH = 4
SB_HEADS = 8
HEAD_DIM = 128
SB_WIDTH = SB_HEADS * HEAD_DIM
SSM_HEAD_DIM = 64
SSM_INNER = 1024
SSM_HEADS = 16
SSM_GROUPS = 2
SSM_STATE = 128
SSM_CONV = 4
SSM_CHUNK = 128
CONV_DIM = SSM_INNER + 2 * SSM_GROUPS * SSM_STATE
DIL_HEADS = 16
DIL_PATTERNS = ((128, 1), (512, 4), (2048, 16))
DIL_BLOCK = 128
D_FF = 5632
FFN_CONV = 3
EPS = 1e-6

LANES = 128
SUBLANES = 8
VMEM_LIMIT = 56 * 1024 * 1024
DT_PAD = LANES
ZXD_WIDTH = SSM_INNER + CONV_DIM + DT_PAD

SB_EXIT_LOG2 = 160.0
LOG2_E = 1.4426950408889634
NEG_BIG = -1e30


def _params(sem, vmem=VMEM_LIMIT):
    return pltpu.CompilerParams(dimension_semantics=sem, vmem_limit_bytes=vmem)


def _rms(x, w):
    ms = jnp.mean(x * x, axis=-1, keepdims=True)
    return x * lax.rsqrt(ms + EPS) * w


def _sigmoid(x):
    return 1.0 / (1.0 + jnp.exp(-x))


def _shift_rows(x, halo, s):
    n = x.shape[0]
    r = pltpu.roll(x, s, axis=0)
    hr = pltpu.roll(halo, s, axis=0)
    rid = lax.broadcasted_iota(jnp.int32, hr.shape, 0)
    top = jnp.where(rid < s, hr, r[:SUBLANES])
    return jnp.concatenate([top, r[SUBLANES:]], axis=0) if n > SUBLANES else top


BF16_ROWS = 2 * SUBLANES


def _cast_specs(cast, steps, step_of):
    in_specs, args, out_specs, out_shape = [], [], [], []
    for w, colscale in cast:
        w = w.reshape(-1, w.shape[-1])
        rows_total, n = w.shape
        share = 1
        while rows_total % (steps // share) or (rows_total // (steps // share)) % BF16_ROWS:
            share *= 2
            assert share <= steps
        rows = rows_total // (steps // share)
        spec = pl.BlockSpec((rows, n), lambda *idx, share=share: (step_of(*idx) // share, 0))
        in_specs.append(spec)
        args.append(w)
        if colscale is not None:
            in_specs.append(pl.BlockSpec((1, n), lambda *idx: (0, 0)))
            args.append(colscale.reshape(1, n))
        out_specs.append(spec)
        out_shape.append(jax.ShapeDtypeStruct((rows_total, n), BF16))
    return in_specs, args, out_specs, out_shape, tuple(s is not None for _, s in cast)


def _convert_blocks(in_refs, out_refs, scaled):
    pos = 0
    for has_scale, wo_ref in zip(scaled, out_refs):
        w = in_refs[pos][...]
        if has_scale:
            w = w * in_refs[pos + 1][...]
        wo_ref[...] = w.astype(BF16)
        pos += 2 if has_scale else 1


SPLIT_ROWS = LANES


def _split_in_proj_kernel(w_ref, dt_ref, qkv_ref, zxd_ref, *, q_blocks, qkv_blocks, w_blocks, scale):
    i = pl.program_id(1)

    @pl.when(i < qkv_blocks)
    def _():
        qkv_ref[...] = (w_ref[...] * jnp.where(i < q_blocks, scale, 1.0)).astype(BF16)

    @pl.when(jnp.logical_and(i >= qkv_blocks, i < w_blocks))
    def _():
        zxd_ref[...] = w_ref[...].astype(BF16)

    @pl.when(i == w_blocks)
    def _():
        zxd_ref[...] = jnp.zeros(zxd_ref.shape, BF16)
        zxd_ref[:dt_ref.shape[0]] = dt_ref[...].astype(BF16)


def split_in_proj(w_in_t, scale):
    s, n, d = w_in_t.shape
    qkv_rows = 3 * SB_WIDTH
    zx_rows = SSM_INNER + CONV_DIM
    rows = SPLIT_ROWS
    assert n == qkv_rows + zx_rows + SSM_HEADS and SB_WIDTH % rows == 0 and zx_rows % rows == 0
    qkv_blocks, w_blocks = qkv_rows // rows, (qkv_rows + zx_rows) // rows
    return pl.pallas_call(
        functools.partial(_split_in_proj_kernel, q_blocks=SB_WIDTH // rows, qkv_blocks=qkv_blocks, w_blocks=w_blocks,
                          scale=scale),
        grid=(s, w_blocks + 1),
        in_specs=[pl.BlockSpec((None, rows, d), lambda si, i: (si, jnp.minimum(i, w_blocks - 1), 0)),
                  pl.BlockSpec((None, SSM_HEADS, d), lambda si, i: (si, (qkv_rows + zx_rows) // SSM_HEADS, 0))],
        out_specs=[pl.BlockSpec((None, rows, d), lambda si, i: (si, jnp.minimum(i, qkv_blocks - 1), 0)),
                   pl.BlockSpec((None, rows, d), lambda si, i: (si, jnp.maximum(i - qkv_blocks, 0), 0))],
        out_shape=[jax.ShapeDtypeStruct((s, qkv_rows, d), BF16), jax.ShapeDtypeStruct((s, ZXD_WIDTH, d), BF16)],
        compiler_params=_params(("parallel", "arbitrary")),
        name="split_in_proj",
    )(w_in_t, w_in_t)


def _norm_matmul_kernel(x_ref, nw_ref, w_ref, o_ref, xn_ref, *, transposed):
    @pl.when(pl.program_id(1) == 0)
    def _():
        xn_ref[...] = _rms(x_ref[...], nw_ref[...]).astype(BF16)

    contract = (((1,), (1 if transposed else 0,)), ((), ()))
    o_ref[...] = lax.dot_general(xn_ref[...], w_ref[...], contract, preferred_element_type=F32).astype(o_ref.dtype)


def norm_matmul(x, nw, w, layer, out_dtype, transposed=False, tm=1024, tn=1536):
    m, d = x.shape
    n = w.shape[1] if transposed else w.shape[2]
    tm, tn = min(tm, m), min(tn, n)
    assert m % tm == 0 and n % tn == 0
    w_spec = (pl.BlockSpec((None, tn, d), lambda i, j: (layer, j, 0)) if transposed
              else pl.BlockSpec((None, d, tn), lambda i, j: (layer, 0, j)))
    return pl.pallas_call(
        functools.partial(_norm_matmul_kernel, transposed=transposed),
        grid=(m // tm, n // tn),
        in_specs=[
            pl.BlockSpec((tm, d), lambda i, j: (i, 0)),
            pl.BlockSpec((1, d), lambda i, j: (0, 0)),
            w_spec,
        ],
        out_specs=pl.BlockSpec((tm, tn), lambda i, j: (i, j)),
        out_shape=jax.ShapeDtypeStruct((m, n), out_dtype),
        scratch_shapes=[pltpu.VMEM((tm, d), BF16)],
        compiler_params=_params(("parallel", "arbitrary")),
        name="norm_matmul",
    )(x, nw.reshape(1, d), w)


def _matmul_residual_kernel(*refs, n_in):
    xs, ws = refs[:n_in], refs[n_in:2 * n_in]
    h_ref, nw_ref, o_ref, xn_ref = refs[2 * n_in:]
    acc = h_ref[...]
    for x_ref, w_ref in zip(xs, ws):
        acc = acc + jnp.dot(x_ref[...], w_ref[...], preferred_element_type=F32)
    o_ref[...] = acc
    xn_ref[...] = _rms(acc, nw_ref[...]).astype(BF16)


def matmul_residual(xs, w, layer, h, nw, tm=512):
    m, n = h.shape
    tm = min(tm, m)
    kx = xs[0].shape[1]
    assert m % tm == 0 and all(x.shape[1] == kx for x in xs) and kx * len(xs) == w.shape[1]
    in_specs = [pl.BlockSpec((tm, kx), lambda i: (i, 0)) for _ in xs]
    in_specs += [pl.BlockSpec((None, kx, n), lambda i, p=p: (layer, p, 0)) for p in range(len(xs))]
    in_specs += [pl.BlockSpec((tm, n), lambda i: (i, 0)), pl.BlockSpec((1, n), lambda i: (0, 0))]
    return pl.pallas_call(
        functools.partial(_matmul_residual_kernel, n_in=len(xs)),
        grid=(m // tm,),
        in_specs=in_specs,
        out_specs=[pl.BlockSpec((tm, n), lambda i: (i, 0)), pl.BlockSpec((tm, n), lambda i: (i, 0))],
        out_shape=[jax.ShapeDtypeStruct((m, n), F32), jax.ShapeDtypeStruct((m, n), BF16)],
        compiler_params=_params(("parallel",)),
        name="matmul_residual",
    )(*xs, *([w] * len(xs)), h, nw.reshape(1, n))


def _conv_ffn_kernel(xn_ref, h_ref, wg_ref, wu_ref, cw_ref, cb_ref, wd_ref, o_ref, act_ref, carry_ref,
                     *, n_f, tf, tiles_per_seq):
    i, j = pl.program_id(0), pl.program_id(1)

    @pl.when(j < n_f)
    def _():
        @pl.when(i % tiles_per_seq == 0)
        def _():
            carry_ref[j] = jnp.zeros(carry_ref.shape[1:], F32)

        xn = xn_ref[...]
        g = jnp.dot(xn, wg_ref[...], preferred_element_type=F32)
        u = jnp.dot(xn, wu_ref[...], preferred_element_type=F32)
        halo = carry_ref[j]
        carry_ref[j] = g[-SUBLANES:]
        cw = cw_ref[...]
        c = cw[2:3] * g + cw[1:2] * _shift_rows(g, halo, 1) + cw[0:1] * _shift_rows(g, halo, 2) + cb_ref[...]
        act_ref[j] = (c * _sigmoid(c) * u).astype(BF16)

    @pl.when(j >= n_f)
    def _():
        acc = h_ref[...]
        for f in range(n_f):
            acc = acc + jnp.dot(act_ref[f], wd_ref[f * tf:(f + 1) * tf, :], preferred_element_type=F32)
        o_ref[...] = acc


def conv_ffn(xn, h, seq_len, wg, wu, cw, cb, wd, layer, tm=1024, tf=512, tn=512):
    m, d = h.shape
    dff = wg.shape[2]
    tm = min(tm, seq_len)
    assert seq_len % tm == 0 and dff % tf == 0 and m % seq_len == 0 and d % tn == 0
    n_f = dff // tf
    gate = lambda i, j: (layer, 0, jnp.minimum(j, n_f - 1))
    down = lambda i, j: (layer, 0, jnp.maximum(j - n_f, 0))
    out = lambda i, j: (i, jnp.maximum(j - n_f, 0))
    return pl.pallas_call(
        functools.partial(_conv_ffn_kernel, n_f=n_f, tf=tf, tiles_per_seq=seq_len // tm),
        grid=(m // tm, n_f + d // tn),
        in_specs=[
            pl.BlockSpec((tm, d), lambda i, j: (i, 0)),
            pl.BlockSpec((tm, tn), out),
            pl.BlockSpec((None, d, tf), gate),
            pl.BlockSpec((None, d, tf), gate),
            pl.BlockSpec((None, FFN_CONV, tf), gate),
            pl.BlockSpec((None, 1, tf), gate),
            pl.BlockSpec((None, dff, tn), down),
        ],
        out_specs=pl.BlockSpec((tm, tn), out),
        out_shape=jax.ShapeDtypeStruct((m, d), F32),
        scratch_shapes=[pltpu.VMEM((n_f, tm, tf), BF16), pltpu.VMEM((n_f, SUBLANES, tf), F32)],
        compiler_params=_params(("arbitrary", "arbitrary")),
        name="conv_ffn",
    )(xn, h, wg, wu, cw, cb.reshape(cb.shape[0], 1, dff), wd)


def _sb_kernel(*refs, tile, heads, scaled):
    q_ref, k_ref, v_ref = refs[:3]
    n_in = 3 + len(scaled) + sum(scaled)
    o_ref = refs[n_in]
    _convert_blocks(refs[3:n_in], refs[n_in + 1:], scaled)
    qi = pl.program_id(2)
    row = lax.broadcasted_iota(jnp.int32, (tile, tile), 0)
    col = lax.broadcasted_iota(jnp.int32, (tile, tile), 1)
    later = (row > col).astype(BF16)
    later2 = jnp.concatenate([later, later], axis=0)
    strictly_before = col < row

    def logits(q, hs, j, diagonal):
        k = k_ref[0, pl.ds(pl.multiple_of(j * tile, tile), tile), hs]
        z = lax.dot_general(q, k, (((1,), (1,)), ((), ())), preferred_element_type=F32)
        sp = jnp.maximum(z, 0.0) + jnp.log2(1.0 + jnp.exp2(-jnp.abs(z)))
        log_keep = -sp
        if diagonal:
            log_keep = jnp.where(strictly_before, log_keep, 0.0)
        hi = log_keep.astype(BF16)
        lo = (log_keep - hi.astype(F32)).astype(BF16)
        suffix = jnp.dot(jnp.concatenate([hi, lo], axis=1), later2, preferred_element_type=F32)
        return (z - sp) + suffix, suffix[:, :1] + log_keep[:, :1]

    def weighted(hs, j, w):
        v = v_ref[0, pl.ds(pl.multiple_of(j * tile, tile), tile), hs]
        return jnp.dot(w.astype(BF16), v, preferred_element_type=F32)

    jb = jnp.maximum(qi - 1, 0)
    no_left_tile = jnp.where(qi > 0, 0.0, NEG_BIG)
    head_slices = [slice(hd * HEAD_DIM, (hd + 1) * HEAD_DIM) for hd in range(heads)]
    first_two = []
    for hs in head_slices:
        q = q_ref[0, :, hs]
        la, ta = logits(q, hs, qi, True)
        lb, tb = logits(q, hs, jb, False)
        acc = weighted(hs, qi, jnp.where(strictly_before, jnp.exp2(la), 0.0))
        acc = acc + weighted(hs, jb, jnp.exp2(lb + (ta + no_left_tile)))
        first_two.append((q, ta + tb, acc))

    for hs, (q, run, acc) in zip(head_slices, first_two):

        def cond(carry):
            j, alive, _, _ = carry
            return jnp.logical_and(j >= 0, alive > 0)

        def body(carry, q=q, hs=hs):
            j, _, run, acc = carry
            lj, tj = logits(q, hs, j, False)
            acc = acc + weighted(hs, j, jnp.exp2(lj + run))
            run = run + tj
            alive = (jnp.max(run) > -SB_EXIT_LOG2).astype(jnp.int32)
            return j - 1, alive, run, acc

        alive0 = (jnp.max(run) > -SB_EXIT_LOG2).astype(jnp.int32)
        _, _, _, acc = lax.while_loop(cond, body, (qi - 2, alive0, run, acc))
        o_ref[0, :, hs] = acc.astype(o_ref.dtype)


def sb_attention(qkv, cast=(), tile=256, heads=2):
    b, t, _ = qkv.shape
    tile = min(tile, t)
    assert t % tile == 0 and SB_HEADS % heads == 0
    groups = SB_HEADS // heads
    width = heads * HEAD_DIM
    nq = t // tile
    steps = b * groups * nq
    in_specs = [
        pl.BlockSpec((1, tile, width), lambda bi, g, i: (bi, i, g)),
        pl.BlockSpec((1, t, width), lambda bi, g, i: (bi, 0, groups + g)),
        pl.BlockSpec((1, t, width), lambda bi, g, i: (bi, 0, 2 * groups + g)),
    ]
    c_in, c_args, c_out, c_shape, scaled = _cast_specs(cast, steps, lambda bi, g, i: (bi * groups + g) * nq + i)
    outs = pl.pallas_call(
        functools.partial(_sb_kernel, tile=tile, heads=heads, scaled=scaled),
        grid=(b, groups, nq),
        in_specs=in_specs + c_in,
        out_specs=[pl.BlockSpec((1, tile, width), lambda bi, g, i: (bi, i, g))] + c_out,
        out_shape=[jax.ShapeDtypeStruct((b, t, SB_WIDTH), BF16)] + c_shape,
        compiler_params=_params(("parallel", "parallel", "arbitrary")),
        name="sb_attention",
    )(qkv, qkv, qkv, *c_args)
    return (outs[0], *[o.reshape(w.shape) for o, (w, _) in zip(outs[1:], cast)])


SSD_INPUTS = 9


def _ssd_kernel(*refs, scaled):
    zxd_ref, halo_ref, cw_ref, cb_ref, dtb_ref, alog_ref, dskip_ref, nw_ref, expand_ref = refs[:SSD_INPUTS]
    n_in = SSD_INPUTS + len(scaled) + sum(scaled)
    o_ref, state_ref = refs[n_in], refs[-1]
    _convert_blocks(refs[SSD_INPUTS:n_in], refs[n_in + 1:-1], scaled)
    c = pl.program_id(1)
    L, N, P = SSM_CHUNK, SSM_STATE, SSM_HEAD_DIM
    heads_per_group = SSM_HEADS // SSM_GROUPS
    gw = heads_per_group * P

    @pl.when(c == 0)
    def _():
        state_ref[...] = jnp.zeros_like(state_ref)

    blk = zxd_ref[0]
    z = blk[:, :SSM_INNER]
    xbc = blk[:, SSM_INNER:SSM_INNER + CONV_DIM]
    dt_raw = blk[:, SSM_INNER + CONV_DIM:]
    halo = jnp.where(c == 0, 0.0, halo_ref[0][:, SSM_INNER:SSM_INNER + CONV_DIM])

    cw = cw_ref[...]
    conv = cw[3:4] * xbc + cb_ref[...]
    for s in range(1, SSM_CONV):
        conv = conv + cw[3 - s:4 - s] * _shift_rows(xbc, halo, s)
    xbc = conv * _sigmoid(conv)
    xs = xbc[:, :SSM_INNER]
    b_in = xbc[:, SSM_INNER:SSM_INNER + SSM_GROUPS * N]
    c_in = xbc[:, SSM_INNER + SSM_GROUPS * N:]

    pre = dt_raw + dtb_ref[...]
    dt = jnp.maximum(pre, 0.0) + jnp.log1p(jnp.exp(-jnp.abs(pre)))
    a = -jnp.exp(alog_ref[...])
    row = lax.broadcasted_iota(jnp.int32, (L, L), 0)
    col = lax.broadcasted_iota(jnp.int32, (L, L), 1)
    causal = row >= col
    tri = causal.astype(F32)
    a_cs = jnp.dot(tri, dt * a, preferred_element_type=F32, precision=lax.Precision.HIGHEST)
    a_cs_t = a_cs.T
    a_end = a_cs[L - 1:L, :]

    expand = expand_ref[...]
    spread = lambda v: jnp.dot(v.astype(BF16), expand[:LANES], preferred_element_type=F32)
    dt_l = spread(dt)
    decay_in_l = spread(jnp.exp(a_cs))
    decay_out_l = spread(jnp.exp(a_end - a_cs))
    cd = jnp.broadcast_to(jnp.exp(a_end), (SUBLANES, LANES))
    cd_hi = cd.astype(BF16)
    cd_r = cd - cd_hi.astype(F32)
    cd_mid = cd_r.astype(BF16)
    cd_lo = (cd_r - cd_mid.astype(F32)).astype(BF16)
    cd_l = jnp.dot(jnp.concatenate([cd_hi, cd_mid, cd_lo], axis=1), expand, preferred_element_type=F32)[:1]

    xdt = xs * dt_l
    xdt_bf = xdt.astype(BF16)
    xdt_out_bf = (xdt * decay_out_l).astype(BF16)

    lane = lax.broadcasted_iota(jnp.int32, (L, LANES), 1)
    first_half = lane < P
    zero = jnp.zeros((L, LANES), BF16)

    ys = []
    for g in range(SSM_GROUPS):
        gs = slice(g * gw, (g + 1) * gw)
        cg = c_in[:, g * N:(g + 1) * N].astype(BF16)
        bg = b_in[:, g * N:(g + 1) * N]
        scores = lax.dot_general(cg, bg.astype(BF16), (((1,), (1,)), ((), ())), preferred_element_type=F32)
        state = state_ref[g]
        y_off = jnp.dot(cg, state.astype(BF16), preferred_element_type=F32) * decay_in_l[:, gs]
        upd = jnp.dot(bg.T.astype(BF16), xdt_out_bf[:, gs], preferred_element_type=F32)
        state_ref[g] = state * cd_l[:, gs] + upd
        for jp in range(heads_per_group // 2):
            ps = slice(g * gw + jp * LANES, g * gw + (jp + 1) * LANES)
            sd = []
            for h in (g * heads_per_group + 2 * jp, g * heads_per_group + 2 * jp + 1):
                decay = jnp.where(causal, jnp.exp(a_cs[:, h:h + 1] - a_cs_t[h:h + 1, :]), 0.0)
                sd.append((scores * decay).astype(BF16))
            xp = xdt_bf[:, ps]
            rhs = jnp.concatenate([jnp.where(first_half, xp, zero), jnp.where(first_half, zero, xp)], axis=0)
            y = jnp.dot(jnp.concatenate(sd, axis=1), rhs, preferred_element_type=F32)
            ys.append(y + y_off[:, jp * LANES:(jp + 1) * LANES] + dskip_ref[:, ps] * xs[:, ps])

    y = jnp.concatenate(ys, axis=1) * (z * _sigmoid(z))
    o_ref[0] = _rms(y, nw_ref[...]).astype(o_ref.dtype)


def ssd(zxd, cw, cb, dt_bias, a_log, d_skip, norm_w, cast=()):
    b, t, width = zxd.shape
    L = SSM_CHUNK
    assert t % L == 0 and width == ZXD_WIDTH
    n_chunks = t // L
    c_in, c_args, c_out, c_shape, scaled = _cast_specs(cast, b * n_chunks, lambda bi, c: bi * n_chunks + c)
    pad = lambda p: jnp.pad(p.astype(F32), (0, DT_PAD - SSM_HEADS)).reshape(1, DT_PAD)
    dskip_lanes = jnp.repeat(d_skip.astype(F32), SSM_HEAD_DIM).reshape(1, SSM_INNER)
    expand = (jnp.arange(LANES)[:, None] == jnp.arange(SSM_INNER)[None, :] // SSM_HEAD_DIM).astype(BF16)
    expand = jnp.concatenate([expand] * 3, axis=0)
    full = lambda shape: pl.BlockSpec(shape, lambda bi, c: (0,) * len(shape))
    in_specs = [
        pl.BlockSpec((1, L, width), lambda bi, c: (bi, c, 0)),
        pl.BlockSpec((1, SUBLANES, width), lambda bi, c: (bi, jnp.maximum(c * (L // SUBLANES) - 1, 0), 0)),
        full((SSM_CONV, CONV_DIM)), full((1, CONV_DIM)), full((1, DT_PAD)), full((1, DT_PAD)),
        full((1, SSM_INNER)), full((1, SSM_INNER)), full((3 * LANES, SSM_INNER)),
    ]
    assert len(in_specs) == SSD_INPUTS
    outs = pl.pallas_call(
        functools.partial(_ssd_kernel, scaled=scaled),
        grid=(b, n_chunks),
        in_specs=in_specs + c_in,
        out_specs=[pl.BlockSpec((1, L, SSM_INNER), lambda bi, c: (bi, c, 0))] + c_out,
        out_shape=[jax.ShapeDtypeStruct((b, t, SSM_INNER), BF16)] + c_shape,
        scratch_shapes=[pltpu.VMEM((SSM_GROUPS, SSM_STATE, (SSM_HEADS // SSM_GROUPS) * SSM_HEAD_DIM), F32)],
        compiler_params=_params(("parallel", "arbitrary")),
        name="ssd",
    )(zxd, zxd, cw, cb.reshape(1, CONV_DIM), pad(dt_bias), pad(a_log), dskip_lanes, norm_w.reshape(1, SSM_INNER),
      expand, *c_args)
    return (outs[0], *[o.reshape(w.shape) for o, (w, _) in zip(outs[1:], cast)])


def _dilated_kernel(q_ref, k_ref, v_ref, o_ref, xf, qd, kd, vd, bias, of, lf, *, span):
    si = pl.program_id(2)
    blk = DIL_BLOCK
    units = span // blk
    cur = si % 2
    prv = 1 - cur

    @pl.when(si == 0)
    def _():
        kd[:, 1] = jnp.zeros(kd.shape[:1] + kd.shape[2:], kd.dtype)
        vd[:, 1] = jnp.zeros(vd.shape[:1] + vd.shape[2:], vd.dtype)
        qpos = lax.broadcasted_iota(jnp.int32, (blk, 2 * blk), 0)
        kpos = lax.broadcasted_iota(jnp.int32, (blk, 2 * blk), 1)
        dist = qpos + blk - kpos
        for bi, (window, r) in enumerate(DIL_PATTERNS):
            band = jnp.logical_and(dist >= 0, dist <= window // r)
            bias[bi, 0] = jnp.where(band, 0.0, NEG_BIG)
            bias[bi, 1] = jnp.where(jnp.logical_and(band, kpos >= blk), 0.0, NEG_BIG)

    xf[0] = q_ref[0].astype(F32)
    xf[1] = k_ref[0].astype(F32)
    xf[2] = v_ref[0].astype(F32)
    for bi, (_, r) in enumerate(DIL_PATTERNS):
        rows = span // r
        if r == 1:
            kd[bi, cur] = k_ref[0]
            vd[bi, cur] = v_ref[0]
            continue
        for c in range(r):
            src = pl.ds(c, rows, stride=r)
            dst = pl.ds(c * rows, rows)
            qd[bi, dst, :] = xf[0, src, :].astype(BF16)
            kd[bi, cur, dst, :] = xf[1, src, :].astype(BF16)
            vd[bi, cur, dst, :] = xf[2, src, :].astype(BF16)

    first = jnp.where(si == 0, 1, 0)
    ones = jnp.ones((2 * blk, HEAD_DIM), BF16)
    for bi, (_, r) in enumerate(DIL_PATTERNS):
        rows = span // r
        for u in range(units):
            cls, nb = divmod(u, units // r)
            base = cls * rows + nb * blk
            here = pl.ds(base, blk)
            q = q_ref[0, here, :] if r == 1 else qd[bi, here, :]
            if nb > 0:
                before = pl.ds(base - blk, blk)
                k_prev, v_prev = kd[bi, cur, before, :], vd[bi, cur, before, :]
                b = bias[bi, 0]
            else:
                before = pl.ds(cls * rows + rows - blk, blk)
                k_prev, v_prev = kd[bi, prv, before, :], vd[bi, prv, before, :]
                b = bias[bi, first]
            k = jnp.concatenate([k_prev, kd[bi, cur, here, :]], axis=0)
            v = jnp.concatenate([v_prev, vd[bi, cur, here, :]], axis=0)
            s = lax.dot_general(q, k, (((1,), (1,)), ((), ())), preferred_element_type=F32) + b
            m = jnp.max(s, axis=-1, keepdims=True)
            p = jnp.exp2(s - m).astype(BF16)
            pv = jnp.dot(p, jnp.concatenate([v, ones], axis=1), preferred_element_type=F32)
            den = pv[:, HEAD_DIM:]
            o = pv[:, :HEAD_DIM] / den
            lse = m + jnp.log2(den)
            natural = pl.ds(base, blk) if r == 1 else pl.ds(cls + r * blk * nb, blk, stride=r)
            of[bi, natural, :] = o
            lf[bi, natural, :] = lse

    l0, l1, l2 = lf[0], lf[1], lf[2]
    m = jnp.maximum(jnp.maximum(l0, l1), l2)
    e0, e1, e2 = jnp.exp2(l0 - m), jnp.exp2(l1 - m), jnp.exp2(l2 - m)
    out = (e0 * of[0] + e1 * of[1] + e2 * of[2]) / (e0 + e1 + e2)
    o_ref[0] = out.astype(o_ref.dtype)


def dilated_attention(qkv):
    b, t, _ = qkv.shape
    span = DIL_BLOCK * max(r for _, r in DIL_PATTERNS)
    assert t % span == 0 and len(DIL_PATTERNS) == 3
    nh, nbr = DIL_HEADS, len(DIL_PATTERNS)
    spec = lambda off: pl.BlockSpec((1, span, HEAD_DIM), lambda bi, h, s: (bi, s, off + h))
    return pl.pallas_call(
        functools.partial(_dilated_kernel, span=span),
        grid=(b, nh, t // span),
        in_specs=[spec(0), spec(nh), spec(2 * nh)],
        out_specs=pl.BlockSpec((1, span, HEAD_DIM), lambda bi, h, s: (bi, s, h)),
        out_shape=jax.ShapeDtypeStruct((b, t, nh * HEAD_DIM), BF16),
        scratch_shapes=[
            pltpu.VMEM((3, span, HEAD_DIM), F32),
            pltpu.VMEM((nbr, span, HEAD_DIM), BF16),
            pltpu.VMEM((nbr, 2, span, HEAD_DIM), BF16),
            pltpu.VMEM((nbr, 2, span, HEAD_DIM), BF16),
            pltpu.VMEM((nbr, 2, DIL_BLOCK, 2 * DIL_BLOCK), F32),
            pltpu.VMEM((nbr, span, HEAD_DIM), F32),
            pltpu.VMEM((nbr, span, HEAD_DIM), F32),
        ],
        compiler_params=_params(("parallel", "parallel", "arbitrary")),
        name="dilated_attention",
    )(qkv, qkv, qkv)


def _rmsnorm_kernel(x_ref, w_ref, o_ref):
    o_ref[...] = _rms(x_ref[...], w_ref[...])


def rmsnorm(x, w, tm=512):
    m, d = x.shape
    tm = min(tm, m)
    assert m % tm == 0
    return pl.pallas_call(
        _rmsnorm_kernel,
        grid=(m // tm,),
        in_specs=[pl.BlockSpec((tm, d), lambda i: (i, 0)), pl.BlockSpec((1, d), lambda i: (0, 0))],
        out_specs=pl.BlockSpec((tm, d), lambda i: (i, 0)),
        out_shape=jax.ShapeDtypeStruct((m, d), F32),
        compiler_params=_params(("parallel",)),
        name="rmsnorm",
    )(x, w.reshape(1, d))


def _q_colscale(n_cols, q_cols):
    return jnp.where(jnp.arange(n_cols) < q_cols, LOG2_E * HEAD_DIM ** -0.5, 1.0).astype(F32)


def _even_mixer(h, bsz, seq, i, nw, ffn_nw, w_qkv_t, w_zxd_t, conv_w, conv_b, dt_bias, a_log, d_skip, ssm_norm_w, w_out,
                ssd_cast=(), sb_cast=()):
    zxd = norm_matmul(h, nw, w_zxd_t, i, F32, transposed=True, tn=ZXD_WIDTH // 3)
    o_b, *ssd_bf16s = ssd(zxd.reshape(bsz, seq, ZXD_WIDTH), conv_w, conv_b, dt_bias, a_log, d_skip, ssm_norm_w,
                          ssd_cast)
    qkv = norm_matmul(h, nw, w_qkv_t, i, BF16, transposed=True)
    o_a, *sb_bf16s = sb_attention(qkv.reshape(bsz, seq, 3 * SB_WIDTH), sb_cast)
    if w_out is None:
        w_out = sb_bf16s[0]
    h, xn = matmul_residual([o_a.reshape(bsz * seq, SB_WIDTH), o_b.reshape(bsz * seq, SSM_INNER)], w_out, i, h,
                            ffn_nw)
    return h, xn, ssd_bf16s, sb_bf16s


def _odd_mixer(h, bsz, seq, i, nw, ffn_nw, w_qkv, w_out):
    qkv = norm_matmul(h, nw, w_qkv, i, BF16)
    o = dilated_attention(qkv.reshape(bsz, seq, 3 * D_MODEL)).reshape(bsz * seq, D_MODEL)
    return matmul_residual([o], w_out, i, h, ffn_nw)


def kernel(x, mix_norm_w, ffn_norm_w, final_norm_w, ev_w_in, ev_conv_w, ev_conv_b, ev_dt_bias, ev_a_log, ev_d_skip, ev_ssm_norm_w, ev_w_out, od_w_in, od_w_out, ffn_w_gate, ffn_w_up, ffn_conv_w, ffn_conv_b, ffn_w_down):
    bsz, seq, d = x.shape
    ev_qkv, ev_zxd = split_in_proj(jnp.swapaxes(ev_w_in, 1, 2), LOG2_E * HEAD_DIM ** -0.5)
    ssd_casts = ((ffn_w_down, None),)
    sb_casts = ((ev_w_out, None), (od_w_in, _q_colscale(od_w_in.shape[-1], D_MODEL)), (od_w_out, None),
                (ffn_w_gate, None), (ffn_w_up, None))
    h = x.reshape(bsz * seq, d)
    for layer in range(DEPTH):
        i = layer // 2
        if layer % 2 == 0:
            first = layer == 0
            h, xn, from_ssd, from_sb = _even_mixer(
                h, bsz, seq, i, mix_norm_w[layer], ffn_norm_w[layer], ev_qkv, ev_zxd, ev_conv_w[i], ev_conv_b[i],
                ev_dt_bias[i], ev_a_log[i], ev_d_skip[i], ev_ssm_norm_w[i], None if first else ev_out,
                ssd_cast=ssd_casts if first else (), sb_cast=sb_casts if first else ())
            if first:
                (w_down,), (ev_out, od_qkv, od_out, w_gate, w_up) = from_ssd, from_sb
        else:
            h, xn = _odd_mixer(h, bsz, seq, i, mix_norm_w[layer], ffn_norm_w[layer], od_qkv, od_out)
        h = conv_ffn(xn, h, seq, w_gate, w_up, ffn_conv_w, ffn_conv_b, w_down, layer)
    return rmsnorm(h, final_norm_w).reshape(bsz, seq, d)
```

```python
import functools

import jax
import jax.numpy as jnp
from jax import lax
from jax.experimental import pallas as pl
from jax.experimental.pallas import tpu as pltpu

F32 = jnp.float32
BF16 = jnp.bfloat16

D_MODEL = 2048
DEPTH = 4
SB_HEADS = 8
HEAD_DIM = 128
SB_WIDTH = SB_HEADS * HEAD_DIM
SSM_HEAD_DIM = 64
SSM_INNER = 1024
SSM_HEADS = 16
SSM_GROUPS = 2
SSM_STATE = 128
SSM_CONV = 4
SSM_CHUNK = 128
CONV_DIM = SSM_INNER + 2 * SSM_GROUPS * SSM_STATE
DIL_HEADS = 16
DIL_PATTERNS = ((128, 1), (512, 4), (2048, 16))
DIL_BLOCK = 128
D_FF = 5632
FFN_CONV = 3
EPS = 1e-6

LANES = 128
SUBLANES = 8
VMEM_LIMIT = 56 * 1024 * 1024
DT_PAD = LANES
ZXD_WIDTH = SSM_INNER + CONV_DIM + DT_PAD

SB_EXIT_LOG2 = 160.0
LOG2_E = 1.4426950408889634
NEG_BIG = -1e30


def _params(sem, vmem=VMEM_LIMIT):
    return pltpu.CompilerParams(dimension_semantics=sem, vmem_limit_bytes=vmem)


def _rms(x, w):
    ms = jnp.mean(x * x, axis=-1, keepdims=True)
    return x * lax.rsqrt(ms + EPS) * w


def _sigmoid(x):
    return 1.0 / (1.0 + jnp.exp(-x))


def _shift_rows(x, halo, s):
    n = x.shape[0]
    r = pltpu.roll(x, s, axis=0)
    hr = pltpu.roll(halo, s, axis=0)
    rid = lax.broadcasted_iota(jnp.int32, hr.shape, 0)
    top = jnp.where(rid < s, hr, r[:SUBLANES])
    return jnp.concatenate([top, r[SUBLANES:]], axis=0) if n > SUBLANES else top


BF16_ROWS = 2 * SUBLANES


def _cast_specs(cast, steps, step_of):
    in_specs, args, out_specs, out_shape = [], [], [], []
    for w, colscale in cast:
        w = w.reshape(-1, w.shape[-1])
        rows_total, n = w.shape
        share = 1
        while rows_total % (steps // share) or (rows_total // (steps // share)) % BF16_ROWS:
            share *= 2
            assert share <= steps
        rows = rows_total // (steps // share)
        spec = pl.BlockSpec((rows, n), lambda *idx, share=share: (step_of(*idx) // share, 0))
        in_specs.append(spec)
        args.append(w)
        if colscale is not None:
            in_specs.append(pl.BlockSpec((1, n), lambda *idx: (0, 0)))
            args.append(colscale.reshape(1, n))
        out_specs.append(spec)
        out_shape.append(jax.ShapeDtypeStruct((rows_total, n), BF16))
    return in_specs, args, out_specs, out_shape, tuple(s is not None for _, s in cast)


def _convert_blocks(in_refs, out_refs, scaled):
    pos = 0
    for has_scale, wo_ref in zip(scaled, out_refs):
        w = in_refs[pos][...]
        if has_scale:
            w = w * in_refs[pos + 1][...]
        wo_ref[...] = w.astype(BF16)
        pos += 2 if has_scale else 1


SPLIT_ROWS = LANES


def _split_in_proj_kernel(w_ref, dt_ref, qkv_ref, zxd_ref, *, q_blocks, qkv_blocks, w_blocks, scale):
    i = pl.program_id(1)

    @pl.when(i < qkv_blocks)
    def _():
        qkv_ref[...] = (w_ref[...] * jnp.where(i < q_blocks, scale, 1.0)).astype(BF16)

    @pl.when(jnp.logical_and(i >= qkv_blocks, i < w_blocks))
    def _():
        zxd_ref[...] = w_ref[...].astype(BF16)

    @pl.when(i == w_blocks)
    def _():
        zxd_ref[...] = jnp.zeros(zxd_ref.shape, BF16)
        zxd_ref[:dt_ref.shape[0]] = dt_ref[...].astype(BF16)


def split_in_proj(w_in_t, scale):
    s, n, d = w_in_t.shape
    qkv_rows = 3 * SB_WIDTH
    zx_rows = SSM_INNER + CONV_DIM
    rows = SPLIT_ROWS
    assert n == qkv_rows + zx_rows + SSM_HEADS and SB_WIDTH % rows == 0 and zx_rows % rows == 0
    qkv_blocks, w_blocks = qkv_rows // rows, (qkv_rows + zx_rows) // rows
    return pl.pallas_call(
        functools.partial(_split_in_proj_kernel, q_blocks=SB_WIDTH // rows, qkv_blocks=qkv_blocks, w_blocks=w_blocks,
                          scale=scale),
        grid=(s, w_blocks + 1),
        in_specs=[pl.BlockSpec((None, rows, d), lambda si, i: (si, jnp.minimum(i, w_blocks - 1), 0)),
                  pl.BlockSpec((None, SSM_HEADS, d), lambda si, i: (si, (qkv_rows + zx_rows) // SSM_HEADS, 0))],
        out_specs=[pl.BlockSpec((None, rows, d), lambda si, i: (si, jnp.minimum(i, qkv_blocks - 1), 0)),
                   pl.BlockSpec((None, rows, d), lambda si, i: (si, jnp.maximum(i - qkv_blocks, 0), 0))],
        out_shape=[jax.ShapeDtypeStruct((s, qkv_rows, d), BF16), jax.ShapeDtypeStruct((s, ZXD_WIDTH, d), BF16)],
        compiler_params=_params(("parallel", "arbitrary")),
        name="split_in_proj",
    )(w_in_t, w_in_t)


def _norm_matmul_kernel(x_ref, nw_ref, w_ref, o_ref, xn_ref, *, transposed):
    @pl.when(pl.program_id(1) == 0)
    def _():
        xn_ref[...] = _rms(x_ref[...], nw_ref[...]).astype(BF16)

    contract = (((1,), (1 if transposed else 0,)), ((), ()))
    o_ref[...] = lax.dot_general(xn_ref[...], w_ref[...], contract, preferred_element_type=F32).astype(o_ref.dtype)


def norm_matmul(x, nw, w, layer, out_dtype, transposed=False, tm=1024, tn=1536):
    m, d = x.shape
    n = w.shape[1] if transposed else w.shape[2]
    tm, tn = min(tm, m), min(tn, n)
    assert m % tm == 0 and n % tn == 0
    w_spec = (pl.BlockSpec((None, tn, d), lambda i, j: (layer, j, 0)) if transposed
              else pl.BlockSpec((None, d, tn), lambda i, j: (layer, 0, j)))
    return pl.pallas_call(
        functools.partial(_norm_matmul_kernel, transposed=transposed),
        grid=(m // tm, n // tn),
        in_specs=[
            pl.BlockSpec((tm, d), lambda i, j: (i, 0)),
            pl.BlockSpec((1, d), lambda i, j: (0, 0)),
            w_spec,
        ],
        out_specs=pl.BlockSpec((tm, tn), lambda i, j: (i, j)),
        out_shape=jax.ShapeDtypeStruct((m, n), out_dtype),
        scratch_shapes=[pltpu.VMEM((tm, d), BF16)],
        compiler_params=_params(("parallel", "arbitrary")),
        name="norm_matmul",
    )(x, nw.reshape(1, d), w)


def _matmul_residual_kernel(*refs, n_in):
    xs, ws = refs[:n_in], refs[n_in:2 * n_in]
    h_ref, nw_ref, o_ref, xn_ref = refs[2 * n_in:]
    acc = h_ref[...]
    for x_ref, w_ref in zip(xs, ws):
        acc = acc + jnp.dot(x_ref[...], w_ref[...], preferred_element_type=F32)
    o_ref[...] = acc
    xn_ref[...] = _rms(acc, nw_ref[...]).astype(BF16)


def matmul_residual(xs, w, layer, h, nw, tm=512):
    m, n = h.shape
    tm = min(tm, m)
    kx = xs[0].shape[1]
    assert m % tm == 0 and all(x.shape[1] == kx for x in xs) and kx * len(xs) == w.shape[1]
    in_specs = [pl.BlockSpec((tm, kx), lambda i: (i, 0)) for _ in xs]
    in_specs += [pl.BlockSpec((None, kx, n), lambda i, p=p: (layer, p, 0)) for p in range(len(xs))]
    in_specs += [pl.BlockSpec((tm, n), lambda i: (i, 0)), pl.BlockSpec((1, n), lambda i: (0, 0))]
    return pl.pallas_call(
        functools.partial(_matmul_residual_kernel, n_in=len(xs)),
        grid=(m // tm,),
        in_specs=in_specs,
        out_specs=[pl.BlockSpec((tm, n), lambda i: (i, 0)), pl.BlockSpec((tm, n), lambda i: (i, 0))],
        out_shape=[jax.ShapeDtypeStruct((m, n), F32), jax.ShapeDtypeStruct((m, n), BF16)],
        compiler_params=_params(("parallel",)),
        name="matmul_residual",
    )(*xs, *([w] * len(xs)), h, nw.reshape(1, n))


def _conv_ffn_kernel(xn_ref, h_ref, wg_ref, wu_ref, cw_ref, cb_ref, wd_ref, o_ref, act_ref, carry_ref,
                     *, n_f, tf, tiles_per_seq):
    i, j = pl.program_id(0), pl.program_id(1)

    @pl.when(j < n_f)
    def _():
        @pl.when(i % tiles_per_seq == 0)
        def _():
            carry_ref[j] = jnp.zeros(carry_ref.shape[1:], F32)

        xn = xn_ref[...]
        g = jnp.dot(xn, wg_ref[...], preferred_element_type=F32)
        u = jnp.dot(xn, wu_ref[...], preferred_element_type=F32)
        halo = carry_ref[j]
        carry_ref[j] = g[-SUBLANES:]
        cw = cw_ref[...]
        c = cw[2:3] * g + cw[1:2] * _shift_rows(g, halo, 1) + cw[0:1] * _shift_rows(g, halo, 2) + cb_ref[...]
        act_ref[j] = (c * _sigmoid(c) * u).astype(BF16)

    @pl.when(j >= n_f)
    def _():
        acc = h_ref[...]
        for f in range(n_f):
            acc = acc + jnp.dot(act_ref[f], wd_ref[f * tf:(f + 1) * tf, :], preferred_element_type=F32)
        o_ref[...] = acc


def conv_ffn(xn, h, seq_len, wg, wu, cw, cb, wd, layer, tm=1024, tf=512, tn=512):
    m, d = h.shape
    dff = wg.shape[2]
    tm = min(tm, seq_len)
    assert seq_len % tm == 0 and dff % tf == 0 and m % seq_len == 0 and d % tn == 0
    n_f = dff // tf
    gate = lambda i, j: (layer, 0, jnp.minimum(j, n_f - 1))
    down = lambda i, j: (layer, 0, jnp.maximum(j - n_f, 0))
    out = lambda i, j: (i, jnp.maximum(j - n_f, 0))
    return pl.pallas_call(
        functools.partial(_conv_ffn_kernel, n_f=n_f, tf=tf, tiles_per_seq=seq_len // tm),
        grid=(m // tm, n_f + d // tn),
        in_specs=[
            pl.BlockSpec((tm, d), lambda i, j: (i, 0)),
            pl.BlockSpec((tm, tn), out),
            pl.BlockSpec((None, d, tf), gate),
            pl.BlockSpec((None, d, tf), gate),
            pl.BlockSpec((None, FFN_CONV, tf), gate),
            pl.BlockSpec((None, 1, tf), gate),
            pl.BlockSpec((None, dff, tn), down),
        ],
        out_specs=pl.BlockSpec((tm, tn), out),
        out_shape=jax.ShapeDtypeStruct((m, d), F32),
        scratch_shapes=[pltpu.VMEM((n_f, tm, tf), BF16), pltpu.VMEM((n_f, SUBLANES, tf), F32)],
        compiler_params=_params(("arbitrary", "arbitrary")),
        name="conv_ffn",
    )(xn, h, wg, wu, cw, cb.reshape(cb.shape[0], 1, dff), wd)


def _sb_kernel(*refs, tile, heads, scaled):
    q_ref, k_ref, v_ref = refs[:3]
    n_in = 3 + len(scaled) + sum(scaled)
    o_ref = refs[n_in]
    _convert_blocks(refs[3:n_in], refs[n_in + 1:], scaled)
    qi = pl.program_id(2)
    row = lax.broadcasted_iota(jnp.int32, (tile, tile), 0)
    col = lax.broadcasted_iota(jnp.int32, (tile, tile), 1)
    later = (row > col).astype(BF16)
    later2 = jnp.concatenate([later, later], axis=0)
    strictly_before = col < row

    def logits(q, hs, j, diagonal):
        k = k_ref[0, pl.ds(pl.multiple_of(j * tile, tile), tile), hs]
        z = lax.dot_general(q, k, (((1,), (1,)), ((), ())), preferred_element_type=F32)
        sp = jnp.maximum(z, 0.0) + jnp.log2(1.0 + jnp.exp2(-jnp.abs(z)))
        log_keep = -sp
        if diagonal:
            log_keep = jnp.where(strictly_before, log_keep, 0.0)
        hi = log_keep.astype(BF16)
        lo = (log_keep - hi.astype(F32)).astype(BF16)
        suffix = jnp.dot(jnp.concatenate([hi, lo], axis=1), later2, preferred_element_type=F32)
        return (z - sp) + suffix, suffix[:, :1] + log_keep[:, :1]

    def weighted(hs, j, w):
        v = v_ref[0, pl.ds(pl.multiple_of(j * tile, tile), tile), hs]
        return jnp.dot(w.astype(BF16), v, preferred_element_type=F32)

    jb = jnp.maximum(qi - 1, 0)
    no_left_tile = jnp.where(qi > 0, 0.0, NEG_BIG)
    head_slices = [slice(hd * HEAD_DIM, (hd + 1) * HEAD_DIM) for hd in range(heads)]
    first_two = []
    for hs in head_slices:
        q = q_ref[0, :, hs]
        la, ta = logits(q, hs, qi, True)
        lb, tb = logits(q, hs, jb, False)
        acc = weighted(hs, qi, jnp.where(strictly_before, jnp.exp2(la), 0.0))
        acc = acc + weighted(hs, jb, jnp.exp2(lb + (ta + no_left_tile)))
        first_two.append((q, ta + tb, acc))

    for hs, (q, run, acc) in zip(head_slices, first_two):

        def cond(carry):
            j, alive, _, _ = carry
            return jnp.logical_and(j >= 0, alive > 0)

        def body(carry, q=q, hs=hs):
            j, _, run, acc = carry
            lj, tj = logits(q, hs, j, False)
            acc = acc + weighted(hs, j, jnp.exp2(lj + run))
            run = run + tj
            alive = (jnp.max(run) > -SB_EXIT_LOG2).astype(jnp.int32)
            return j - 1, alive, run, acc

        alive0 = (jnp.max(run) > -SB_EXIT_LOG2).astype(jnp.int32)
        _, _, _, acc = lax.while_loop(cond, body, (qi - 2, alive0, run, acc))
        o_ref[0, :, hs] = acc.astype(o_ref.dtype)


def sb_attention(qkv, cast=(), tile=256, heads=2):
    b, t, _ = qkv.shape
    tile = min(tile, t)
    assert t % tile == 0 and SB_HEADS % heads == 0
    groups = SB_HEADS // heads
    width = heads * HEAD_DIM
    nq = t // tile
    steps = b * groups * nq
    in_specs = [
        pl.BlockSpec((1, tile, width), lambda bi, g, i: (bi, i, g)),
        pl.BlockSpec((1, t, width), lambda bi, g, i: (bi, 0, groups + g)),
        pl.BlockSpec((1, t, width), lambda bi, g, i: (bi, 0, 2 * groups + g)),
    ]
    c_in, c_args, c_out, c_shape, scaled = _cast_specs(cast, steps, lambda bi, g, i: (bi * groups + g) * nq + i)
    outs = pl.pallas_call(
        functools.partial(_sb_kernel, tile=tile, heads=heads, scaled=scaled),
        grid=(b, groups, nq),
        in_specs=in_specs + c_in,
        out_specs=[pl.BlockSpec((1, tile, width), lambda bi, g, i: (bi, i, g))] + c_out,
        out_shape=[jax.ShapeDtypeStruct((b, t, SB_WIDTH), BF16)] + c_shape,
        compiler_params=_params(("parallel", "parallel", "arbitrary")),
        name="sb_attention",
    )(qkv, qkv, qkv, *c_args)
    return (outs[0], *[o.reshape(w.shape) for o, (w, _) in zip(outs[1:], cast)])


SSD_INPUTS = 9


def _ssd_kernel(*refs, scaled):
    zxd_ref, halo_ref, cw_ref, cb_ref, dtb_ref, alog_ref, dskip_ref, nw_ref, expand_ref = refs[:SSD_INPUTS]
    n_in = SSD_INPUTS + len(scaled) + sum(scaled)
    o_ref, state_ref = refs[n_in], refs[-1]
    _convert_blocks(refs[SSD_INPUTS:n_in], refs[n_in + 1:-1], scaled)
    c = pl.program_id(1)
    L, N, P = SSM_CHUNK, SSM_STATE, SSM_HEAD_DIM
    heads_per_group = SSM_HEADS // SSM_GROUPS
    gw = heads_per_group * P

    @pl.when(c == 0)
    def _():
        state_ref[...] = jnp.zeros_like(state_ref)

    blk = zxd_ref[0]
    z = blk[:, :SSM_INNER]
    xbc = blk[:, SSM_INNER:SSM_INNER + CONV_DIM]
    dt_raw = blk[:, SSM_INNER + CONV_DIM:]
    halo = jnp.where(c == 0, 0.0, halo_ref[0][:, SSM_INNER:SSM_INNER + CONV_DIM])

    cw = cw_ref[...]
    conv = cw[3:4] * xbc + cb_ref[...]
    for s in range(1, SSM_CONV):
        conv = conv + cw[3 - s:4 - s] * _shift_rows(xbc, halo, s)
    xbc = conv * _sigmoid(conv)
    xs = xbc[:, :SSM_INNER]
    b_in = xbc[:, SSM_INNER:SSM_INNER + SSM_GROUPS * N]
    c_in = xbc[:, SSM_INNER + SSM_GROUPS * N:]

    pre = dt_raw + dtb_ref[...]
    dt = jnp.maximum(pre, 0.0) + jnp.log1p(jnp.exp(-jnp.abs(pre)))
    a = -jnp.exp(alog_ref[...])
    row = lax.broadcasted_iota(jnp.int32, (L, L), 0)
    col = lax.broadcasted_iota(jnp.int32, (L, L), 1)
    causal = row >= col
    tri = causal.astype(F32)
    a_cs = jnp.dot(tri, dt * a, preferred_element_type=F32, precision=lax.Precision.HIGHEST)
    a_cs_t = a_cs.T
    a_end = a_cs[L - 1:L, :]

    expand = expand_ref[...]
    spread = lambda v: jnp.dot(v.astype(BF16), expand[:LANES], preferred_element_type=F32)
    dt_l = spread(dt)
    decay_in_l = spread(jnp.exp(a_cs))
    decay_out_l = spread(jnp.exp(a_end - a_cs))
    cd = jnp.broadcast_to(jnp.exp(a_end), (SUBLANES, LANES))
    cd_hi = cd.astype(BF16)
    cd_r = cd - cd_hi.astype(F32)
    cd_mid = cd_r.astype(BF16)
    cd_lo = (cd_r - cd_mid.astype(F32)).astype(BF16)
    cd_l = jnp.dot(jnp.concatenate([cd_hi, cd_mid, cd_lo], axis=1), expand, preferred_element_type=F32)[:1]

    xdt = xs * dt_l
    xdt_bf = xdt.astype(BF16)
    xdt_out_bf = (xdt * decay_out_l).astype(BF16)

    lane = lax.broadcasted_iota(jnp.int32, (L, LANES), 1)
    first_half = lane < P
    zero = jnp.zeros((L, LANES), BF16)

    ys = []
    for g in range(SSM_GROUPS):
        gs = slice(g * gw, (g + 1) * gw)
        cg = c_in[:, g * N:(g + 1) * N].astype(BF16)
        bg = b_in[:, g * N:(g + 1) * N]
        scores = lax.dot_general(cg, bg.astype(BF16), (((1,), (1,)), ((), ())), preferred_element_type=F32)
        state = state_ref[g]
        y_off = jnp.dot(cg, state.astype(BF16), preferred_element_type=F32) * decay_in_l[:, gs]
        upd = jnp.dot(bg.T.astype(BF16), xdt_out_bf[:, gs], preferred_element_type=F32)
        state_ref[g] = state * cd_l[:, gs] + upd
        for jp in range(heads_per_group // 2):
            ps = slice(g * gw + jp * LANES, g * gw + (jp + 1) * LANES)
            sd = []
            for h in (g * heads_per_group + 2 * jp, g * heads_per_group + 2 * jp + 1):
                decay = jnp.where(causal, jnp.exp(a_cs[:, h:h + 1] - a_cs_t[h:h + 1, :]), 0.0)
                sd.append((scores * decay).astype(BF16))
            xp = xdt_bf[:, ps]
            rhs = jnp.concatenate([jnp.where(first_half, xp, zero), jnp.where(first_half, zero, xp)], axis=0)
            y = jnp.dot(jnp.concatenate(sd, axis=1), rhs, preferred_element_type=F32)
            ys.append(y + y_off[:, jp * LANES:(jp + 1) * LANES] + dskip_ref[:, ps] * xs[:, ps])

    y = jnp.concatenate(ys, axis=1) * (z * _sigmoid(z))
    o_ref[0] = _rms(y, nw_ref[...]).astype(o_ref.dtype)


def ssd(zxd, cw, cb, dt_bias, a_log, d_skip, norm_w, cast=()):
    b, t, width = zxd.shape
    L = SSM_CHUNK
    assert t % L == 0 and width == ZXD_WIDTH
    n_chunks = t // L
    c_in, c_args, c_out, c_shape, scaled = _cast_specs(cast, b * n_chunks, lambda bi, c: bi * n_chunks + c)
    pad = lambda p: jnp.pad(p.astype(F32), (0, DT_PAD - SSM_HEADS)).reshape(1, DT_PAD)
    dskip_lanes = jnp.repeat(d_skip.astype(F32), SSM_HEAD_DIM).reshape(1, SSM_INNER)
    expand = (jnp.arange(LANES)[:, None] == jnp.arange(SSM_INNER)[None, :] // SSM_HEAD_DIM).astype(BF16)
    expand = jnp.concatenate([expand] * 3, axis=0)
    full = lambda shape: pl.BlockSpec(shape, lambda bi, c: (0,) * len(shape))
    in_specs = [
        pl.BlockSpec((1, L, width), lambda bi, c: (bi, c, 0)),
        pl.BlockSpec((1, SUBLANES, width), lambda bi, c: (bi, jnp.maximum(c * (L // SUBLANES) - 1, 0), 0)),
        full((SSM_CONV, CONV_DIM)), full((1, CONV_DIM)), full((1, DT_PAD)), full((1, DT_PAD)),
        full((1, SSM_INNER)), full((1, SSM_INNER)), full((3 * LANES, SSM_INNER)),
    ]
    assert len(in_specs) == SSD_INPUTS
    outs = pl.pallas_call(
        functools.partial(_ssd_kernel, scaled=scaled),
        grid=(b, n_chunks),
        in_specs=in_specs + c_in,
        out_specs=[pl.BlockSpec((1, L, SSM_INNER), lambda bi, c: (bi, c, 0))] + c_out,
        out_shape=[jax.ShapeDtypeStruct((b, t, SSM_INNER), BF16)] + c_shape,
        scratch_shapes=[pltpu.VMEM((SSM_GROUPS, SSM_STATE, (SSM_HEADS // SSM_GROUPS) * SSM_HEAD_DIM), F32)],
        compiler_params=_params(("parallel", "arbitrary")),
        name="ssd",
    )(zxd, zxd, cw, cb.reshape(1, CONV_DIM), pad(dt_bias), pad(a_log), dskip_lanes, norm_w.reshape(1, SSM_INNER),
      expand, *c_args)
    return (outs[0], *[o.reshape(w.shape) for o, (w, _) in zip(outs[1:], cast)])


def _dilated_kernel(q_ref, k_ref, v_ref, o_ref, xf, qd, kd_all, vd_all, bias, of, lf, *, span, heads):
    si = pl.program_id(2)
    blk = DIL_BLOCK
    units = span // blk
    cur = si % 2
    prv = 1 - cur

    @pl.when(si == 0)
    def _():
        kd_all[:, :, 1] = jnp.zeros(kd_all.shape[:2] + kd_all.shape[3:], kd_all.dtype)
        vd_all[:, :, 1] = jnp.zeros(vd_all.shape[:2] + vd_all.shape[3:], vd_all.dtype)
        qpos = lax.broadcasted_iota(jnp.int32, (blk, 2 * blk), 0)
        kpos = lax.broadcasted_iota(jnp.int32, (blk, 2 * blk), 1)
        dist = qpos + blk - kpos
        for bi, (window, r) in enumerate(DIL_PATTERNS):
            band = jnp.logical_and(dist >= 0, dist <= window // r)
            bias[bi, 0] = jnp.where(band, 0.0, NEG_BIG)
            bias[bi, 1] = jnp.where(jnp.logical_and(band, kpos >= blk), 0.0, NEG_BIG)

    first = jnp.where(si == 0, 1, 0)
    ones = jnp.ones((2 * blk, HEAD_DIM), BF16)
    for hd in range(heads):
        hs = pl.ds(hd * HEAD_DIM, HEAD_DIM)
        _dilated_head(q_ref.at[0, :, hs], k_ref.at[0, :, hs], v_ref.at[0, :, hs], o_ref.at[0, :, hs], xf, qd,
                      kd_all.at[hd], vd_all.at[hd], bias, of, lf, span, cur, prv, first, ones)


def _dilated_head(q_ref, k_ref, v_ref, o_ref, xf, qd, kd, vd, bias, of, lf, span, cur, prv, first, ones):
    blk = DIL_BLOCK
    units = span // blk
    xf[0] = q_ref[...].astype(F32)
    xf[1] = k_ref[...].astype(F32)
    xf[2] = v_ref[...].astype(F32)
    for bi, (_, r) in enumerate(DIL_PATTERNS):
        rows = span // r
        if r == 1:
            kd[bi, cur] = k_ref[...]
            vd[bi, cur] = v_ref[...]
            continue
        for c in range(r):
            src = pl.ds(c, rows, stride=r)
            dst = pl.ds(c * rows, rows)
            qd[bi, dst, :] = xf[0, src, :].astype(BF16)
            kd[bi, cur, dst, :] = xf[1, src, :].astype(BF16)
            vd[bi, cur, dst, :] = xf[2, src, :].astype(BF16)

    for bi, (_, r) in enumerate(DIL_PATTERNS):
        rows = span // r
        for u in range(units):
            cls, nb = divmod(u, units // r)
            base = cls * rows + nb * blk
            here = pl.ds(base, blk)
            q = q_ref[here, :] if r == 1 else qd[bi, here, :]
            if nb > 0:
                before = pl.ds(base - blk, blk)
                k_prev, v_prev = kd[bi, cur, before, :], vd[bi, cur, before, :]
                b = bias[bi, 0]
            else:
                before = pl.ds(cls * rows + rows - blk, blk)
                k_prev, v_prev = kd[bi, prv, before, :], vd[bi, prv, before, :]
                b = bias[bi, first]
            k = jnp.concatenate([k_prev, kd[bi, cur, here, :]], axis=0)
            v = jnp.concatenate([v_prev, vd[bi, cur, here, :]], axis=0)
            s = lax.dot_general(q, k, (((1,), (1,)), ((), ())), preferred_element_type=F32) + b
            m = jnp.max(s, axis=-1, keepdims=True)
            p = jnp.exp2(s - m).astype(BF16)
            pv = jnp.dot(p, jnp.concatenate([v, ones], axis=1), preferred_element_type=F32)
            den = pv[:, HEAD_DIM:]
            o = pv[:, :HEAD_DIM] / den
            lse = m + jnp.log2(den)
            natural = pl.ds(base, blk) if r == 1 else pl.ds(cls + r * blk * nb, blk, stride=r)
            of[bi, natural, :] = o
            lf[bi, natural, :] = lse

    l0, l1, l2 = lf[0], lf[1], lf[2]
    m = jnp.maximum(jnp.maximum(l0, l1), l2)
    e0, e1, e2 = jnp.exp2(l0 - m), jnp.exp2(l1 - m), jnp.exp2(l2 - m)
    out = (e0 * of[0] + e1 * of[1] + e2 * of[2]) / (e0 + e1 + e2)
    o_ref[...] = out.astype(o_ref.dtype)


def dilated_attention(qkv, heads=2):
    b, t, _ = qkv.shape
    span = DIL_BLOCK * max(r for _, r in DIL_PATTERNS)
    assert t % span == 0 and len(DIL_PATTERNS) == 3 and DIL_HEADS % heads == 0
    nh, nbr = DIL_HEADS, len(DIL_PATTERNS)
    groups = nh // heads
    width = heads * HEAD_DIM
    spec = lambda off: pl.BlockSpec((1, span, width), lambda bi, g, s: (bi, s, off + g))
    return pl.pallas_call(
        functools.partial(_dilated_kernel, span=span, heads=heads),
        grid=(b, groups, t // span),
        in_specs=[spec(0), spec(groups), spec(2 * groups)],
        out_specs=pl.BlockSpec((1, span, width), lambda bi, g, s: (bi, s, g)),
        out_shape=jax.ShapeDtypeStruct((b, t, nh * HEAD_DIM), BF16),
        scratch_shapes=[
            pltpu.VMEM((3, span, HEAD_DIM), F32),
            pltpu.VMEM((nbr, span, HEAD_DIM), BF16),
            pltpu.VMEM((heads, nbr, 2, span, HEAD_DIM), BF16),
            pltpu.VMEM((heads, nbr, 2, span, HEAD_DIM), BF16),
            pltpu.VMEM((nbr, 2, DIL_BLOCK, 2 * DIL_BLOCK), F32),
            pltpu.VMEM((nbr, span, HEAD_DIM), F32),
            pltpu.VMEM((nbr, span, HEAD_DIM), F32),
        ],
        compiler_params=_params(("parallel", "parallel", "arbitrary")),
        name="dilated_attention",
    )(qkv, qkv, qkv)


def _rmsnorm_kernel(x_ref, w_ref, o_ref):
    o_ref[...] = _rms(x_ref[...], w_ref[...])


def rmsnorm(x, w, tm=512):
    m, d = x.shape
    tm = min(tm, m)
    assert m % tm == 0
    return pl.pallas_call(
        _rmsnorm_kernel,
        grid=(m // tm,),
        in_specs=[pl.BlockSpec((tm, d), lambda i: (i, 0)), pl.BlockSpec((1, d), lambda i: (0, 0))],
        out_specs=pl.BlockSpec((tm, d), lambda i: (i, 0)),
        out_shape=jax.ShapeDtypeStruct((m, d), F32),
        compiler_params=_params(("parallel",)),
        name="rmsnorm",
    )(x, w.reshape(1, d))


def _q_colscale(n_cols, q_cols):
    return jnp.where(jnp.arange(n_cols) < q_cols, LOG2_E * HEAD_DIM ** -0.5, 1.0).astype(F32)


def _even_mixer(h, bsz, seq, i, nw, ffn_nw, w_qkv_t, w_zxd_t, conv_w, conv_b, dt_bias, a_log, d_skip, ssm_norm_w, w_out,
                ssd_cast=(), sb_cast=()):
    zxd = norm_matmul(h, nw, w_zxd_t, i, F32, transposed=True, tn=ZXD_WIDTH // 3)
    o_b, *ssd_bf16s = ssd(zxd.reshape(bsz, seq, ZXD_WIDTH), conv_w, conv_b, dt_bias, a_log, d_skip, ssm_norm_w,
                          ssd_cast)
    qkv = norm_matmul(h, nw, w_qkv_t, i, BF16, transposed=True)
    o_a, *sb_bf16s = sb_attention(qkv.reshape(bsz, seq, 3 * SB_WIDTH), sb_cast)
    if w_out is None:
        w_out = sb_bf16s[0]
    h, xn = matmul_residual([o_a.reshape(bsz * seq, SB_WIDTH), o_b.reshape(bsz * seq, SSM_INNER)], w_out, i, h,
                            ffn_nw)
    return h, xn, ssd_bf16s, sb_bf16s


def _odd_mixer(h, bsz, seq, i, nw, ffn_nw, w_qkv, w_out):
    qkv = norm_matmul(h, nw, w_qkv, i, BF16)
    o = dilated_attention(qkv.reshape(bsz, seq, 3 * D_MODEL)).reshape(bsz * seq, D_MODEL)
    return matmul_residual([o], w_out, i, h, ffn_nw)


def kernel(x, mix_norm_w, ffn_norm_w, final_norm_w, ev_w_in, ev_conv_w, ev_conv_b, ev_dt_bias, ev_a_log, ev_d_skip, ev_ssm_norm_w, ev_w_out, od_w_in, od_w_out, ffn_w_gate, ffn_w_up, ffn_conv_w, ffn_conv_b, ffn_w_down):
    bsz, seq, d = x.shape
    ev_qkv, ev_zxd = split_in_proj(jnp.swapaxes(ev_w_in, 1, 2), LOG2_E * HEAD_DIM ** -0.5)
    ssd_casts = ((ffn_w_down, None),)
    sb_casts = ((ev_w_out, None), (od_w_in, _q_colscale(od_w_in.shape[-1], D_MODEL)), (od_w_out, None),
                (ffn_w_gate, None), (ffn_w_up, None))
    h = x.reshape(bsz * seq, d)
    for layer in range(DEPTH):
        i = layer // 2
        if layer % 2 == 0:
            first = layer == 0
            h, xn, from_ssd, from_sb = _even_mixer(
                h, bsz, seq, i, mix_norm_w[layer], ffn_norm_w[layer], ev_qkv, ev_zxd, ev_conv_w[i], ev_conv_b[i],
                ev_dt_bias[i], ev_a_log[i], ev_d_skip[i], ev_ssm_norm_w[i], None if first else ev_out,
                ssd_cast=ssd_casts if first else (), sb_cast=sb_casts if first else ())
            if first:
                (w_down,), (ev_out, od_qkv, od_out, w_gate, w_up) = from_ssd, from_sb
        else:
            h, xn = _odd_mixer(h, bsz, seq, i, mix_norm_w[layer], ffn_norm_w[layer], od_qkv, od_out)
        h = conv_ffn(xn, h, seq, w_gate, w_up, ffn_conv_w, ffn_conv_b, w_down, layer)
    return rmsnorm(h, final_norm_w).reshape(bsz, seq, d)
```

```python
import functools

import jax
import jax.numpy as jnp
from jax import lax
from jax.experimental import pallas as pl
from jax.experimental.pallas import tpu as pltpu

F32 = jnp.float32
BF16 = jnp.bfloat16

D_MODEL = 2048
DEPTH = 4
SB_HEADS = 8
HEAD_DIM = 128
SB_WIDTH = SB_HEADS * HEAD_DIM
SSM_HEAD_DIM = 64
SSM_INNER = 1024
SSM_HEADS = 16
SSM_GROUPS = 2
SSM_STATE = 128
SSM_CONV = 4
SSM_CHUNK = 128
CONV_DIM = SSM_INNER + 2 * SSM_GROUPS * SSM_STATE
DIL_HEADS = 16
DIL_PATTERNS = ((128, 1), (512, 4), (2048, 16))
DIL_BLOCK = 128
D_FF = 5632
FFN_CONV = 3
EPS = 1e-6

LANES = 128
SUBLANES = 8
VMEM_LIMIT = 56 * 1024 * 1024
DT_PAD = LANES
ZXD_WIDTH = SSM_INNER + CONV_DIM + DT_PAD

SB_EXIT_LOG2 = 160.0
LOG2_E = 1.4426950408889634
NEG_BIG = -1e30


def _params(sem, vmem=VMEM_LIMIT):
    return pltpu.CompilerParams(dimension_semantics=sem, vmem_limit_bytes=vmem)


def _rms(x, w):
    ms = jnp.mean(x * x, axis=-1, keepdims=True)
    return x * lax.rsqrt(ms + EPS) * w


def _sigmoid(x):
    return 1.0 / (1.0 + jnp.exp(-x))


def _shift_rows(x, halo, s):
    n = x.shape[0]
    r = pltpu.roll(x, s, axis=0)
    hr = pltpu.roll(halo, s, axis=0)
    rid = lax.broadcasted_iota(jnp.int32, hr.shape, 0)
    top = jnp.where(rid < s, hr, r[:SUBLANES])
    return jnp.concatenate([top, r[SUBLANES:]], axis=0) if n > SUBLANES else top


BF16_ROWS = 2 * SUBLANES


def _cast_specs(cast, steps, step_of):
    in_specs, args, out_specs, out_shape = [], [], [], []
    for w, colscale in cast:
        w = w.reshape(-1, w.shape[-1])
        rows_total, n = w.shape
        share = 1
        while rows_total % (steps // share) or (rows_total // (steps // share)) % BF16_ROWS:
            share *= 2
            assert share <= steps
        rows = rows_total // (steps // share)
        spec = pl.BlockSpec((rows, n), lambda *idx, share=share: (step_of(*idx) // share, 0))
        in_specs.append(spec)
        args.append(w)
        if colscale is not None:
            in_specs.append(pl.BlockSpec((1, n), lambda *idx: (0, 0)))
            args.append(colscale.reshape(1, n))
        out_specs.append(spec)
        out_shape.append(jax.ShapeDtypeStruct((rows_total, n), BF16))
    return in_specs, args, out_specs, out_shape, tuple(s is not None for _, s in cast)


def _convert_blocks(in_refs, out_refs, scaled):
    pos = 0
    for has_scale, wo_ref in zip(scaled, out_refs):
        w = in_refs[pos][...]
        if has_scale:
            w = w * in_refs[pos + 1][...]
        wo_ref[...] = w.astype(BF16)
        pos += 2 if has_scale else 1


SPLIT_ROWS = LANES


def _split_in_proj_kernel(w_ref, dt_ref, qkv_ref, zxd_ref, *, q_blocks, qkv_blocks, w_blocks, scale):
    i = pl.program_id(1)

    @pl.when(i < qkv_blocks)
    def _():
        qkv_ref[...] = (w_ref[...] * jnp.where(i < q_blocks, scale, 1.0)).astype(BF16)

    @pl.when(jnp.logical_and(i >= qkv_blocks, i < w_blocks))
    def _():
        zxd_ref[...] = w_ref[...].astype(BF16)

    @pl.when(i == w_blocks)
    def _():
        zxd_ref[...] = jnp.zeros(zxd_ref.shape, BF16)
        zxd_ref[:dt_ref.shape[0]] = dt_ref[...].astype(BF16)


def split_in_proj(w_in_t, scale):
    s, n, d = w_in_t.shape
    qkv_rows = 3 * SB_WIDTH
    zx_rows = SSM_INNER + CONV_DIM
    rows = SPLIT_ROWS
    assert n == qkv_rows + zx_rows + SSM_HEADS and SB_WIDTH % rows == 0 and zx_rows % rows == 0
    qkv_blocks, w_blocks = qkv_rows // rows, (qkv_rows + zx_rows) // rows
    return pl.pallas_call(
        functools.partial(_split_in_proj_kernel, q_blocks=SB_WIDTH // rows, qkv_blocks=qkv_blocks, w_blocks=w_blocks,
                          scale=scale),
        grid=(s, w_blocks + 1),
        in_specs=[pl.BlockSpec((None, rows, d), lambda si, i: (si, jnp.minimum(i, w_blocks - 1), 0)),
                  pl.BlockSpec((None, SSM_HEADS, d), lambda si, i: (si, (qkv_rows + zx_rows) // SSM_HEADS, 0))],
        out_specs=[pl.BlockSpec((None, rows, d), lambda si, i: (si, jnp.minimum(i, qkv_blocks - 1), 0)),
                   pl.BlockSpec((None, rows, d), lambda si, i: (si, jnp.maximum(i - qkv_blocks, 0), 0))],
        out_shape=[jax.ShapeDtypeStruct((s, qkv_rows, d), BF16), jax.ShapeDtypeStruct((s, ZXD_WIDTH, d), BF16)],
        compiler_params=_params(("parallel", "arbitrary")),
        name="split_in_proj",
    )(w_in_t, w_in_t)


def _norm_matmul_kernel(x_ref, nw_ref, w_ref, o_ref, xn_ref, *, transposed):
    @pl.when(pl.program_id(1) == 0)
    def _():
        xn_ref[...] = _rms(x_ref[...], nw_ref[...]).astype(BF16)

    contract = (((1,), (1 if transposed else 0,)), ((), ()))
    o_ref[...] = lax.dot_general(xn_ref[...], w_ref[...], contract, preferred_element_type=F32).astype(o_ref.dtype)


def norm_matmul(x, nw, w, layer, out_dtype, transposed=False, tm=1024, tn=1536):
    m, d = x.shape
    n = w.shape[1] if transposed else w.shape[2]
    tm, tn = min(tm, m), min(tn, n)
    assert m % tm == 0 and n % tn == 0
    w_spec = (pl.BlockSpec((None, tn, d), lambda i, j: (layer, j, 0)) if transposed
              else pl.BlockSpec((None, d, tn), lambda i, j: (layer, 0, j)))
    return pl.pallas_call(
        functools.partial(_norm_matmul_kernel, transposed=transposed),
        grid=(m // tm, n // tn),
        in_specs=[
            pl.BlockSpec((tm, d), lambda i, j: (i, 0)),
            pl.BlockSpec((1, d), lambda i, j: (0, 0)),
            w_spec,
        ],
        out_specs=pl.BlockSpec((tm, tn), lambda i, j: (i, j)),
        out_shape=jax.ShapeDtypeStruct((m, n), out_dtype),
        scratch_shapes=[pltpu.VMEM((tm, d), BF16)],
        compiler_params=_params(("parallel", "arbitrary")),
        name="norm_matmul",
    )(x, nw.reshape(1, d), w)


def _matmul_residual_kernel(*refs, n_in):
    xs, ws = refs[:n_in], refs[n_in:2 * n_in]
    h_ref, nw_ref, o_ref, xn_ref = refs[2 * n_in:]
    acc = h_ref[...]
    for x_ref, w_ref in zip(xs, ws):
        acc = acc + jnp.dot(x_ref[...], w_ref[...], preferred_element_type=F32)
    o_ref[...] = acc
    xn_ref[...] = _rms(acc, nw_ref[...]).astype(BF16)


def matmul_residual(xs, w, layer, h, nw, tm=512):
    m, n = h.shape
    tm = min(tm, m)
    kx = xs[0].shape[1]
    assert m % tm == 0 and all(x.shape[1] == kx for x in xs) and kx * len(xs) == w.shape[1]
    in_specs = [pl.BlockSpec((tm, kx), lambda i: (i, 0)) for _ in xs]
    in_specs += [pl.BlockSpec((None, kx, n), lambda i, p=p: (layer, p, 0)) for p in range(len(xs))]
    in_specs += [pl.BlockSpec((tm, n), lambda i: (i, 0)), pl.BlockSpec((1, n), lambda i: (0, 0))]
    return pl.pallas_call(
        functools.partial(_matmul_residual_kernel, n_in=len(xs)),
        grid=(m // tm,),
        in_specs=in_specs,
        out_specs=[pl.BlockSpec((tm, n), lambda i: (i, 0)), pl.BlockSpec((tm, n), lambda i: (i, 0))],
        out_shape=[jax.ShapeDtypeStruct((m, n), F32), jax.ShapeDtypeStruct((m, n), BF16)],
        compiler_params=_params(("parallel",)),
        name="matmul_residual",
    )(*xs, *([w] * len(xs)), h, nw.reshape(1, n))


def _conv_ffn_kernel(xn_ref, h_ref, wg_ref, wu_ref, cw_ref, cb_ref, wd_ref, o_ref, act_ref, carry_ref,
                     *, n_f, tf, tiles_per_seq):
    i, j = pl.program_id(0), pl.program_id(1)

    @pl.when(j < n_f)
    def _():
        @pl.when(i % tiles_per_seq == 0)
        def _():
            carry_ref[j] = jnp.zeros(carry_ref.shape[1:], F32)

        xn = xn_ref[...]
        g = jnp.dot(xn, wg_ref[...], preferred_element_type=F32)
        u = jnp.dot(xn, wu_ref[...], preferred_element_type=F32)
        halo = carry_ref[j]
        carry_ref[j] = g[-SUBLANES:]
        cw = cw_ref[...]
        c = cw[2:3] * g + cw[1:2] * _shift_rows(g, halo, 1) + cw[0:1] * _shift_rows(g, halo, 2) + cb_ref[...]
        act_ref[j] = (c * _sigmoid(c) * u).astype(BF16)

    @pl.when(j >= n_f)
    def _():
        acc = h_ref[...]
        for f in range(n_f):
            acc = acc + jnp.dot(act_ref[f], wd_ref[f * tf:(f + 1) * tf, :], preferred_element_type=F32)
        o_ref[...] = acc


def conv_ffn(xn, h, seq_len, wg, wu, cw, cb, wd, layer, tm=1024, tf=512, tn=512):
    m, d = h.shape
    dff = wg.shape[2]
    tm = min(tm, seq_len)
    assert seq_len % tm == 0 and dff % tf == 0 and m % seq_len == 0 and d % tn == 0
    n_f = dff // tf
    gate = lambda i, j: (layer, 0, jnp.minimum(j, n_f - 1))
    down = lambda i, j: (layer, 0, jnp.maximum(j - n_f, 0))
    out = lambda i, j: (i, jnp.maximum(j - n_f, 0))
    return pl.pallas_call(
        functools.partial(_conv_ffn_kernel, n_f=n_f, tf=tf, tiles_per_seq=seq_len // tm),
        grid=(m // tm, n_f + d // tn),
        in_specs=[
            pl.BlockSpec((tm, d), lambda i, j: (i, 0)),
            pl.BlockSpec((tm, tn), out),
            pl.BlockSpec((None, d, tf), gate),
            pl.BlockSpec((None, d, tf), gate),
            pl.BlockSpec((None, FFN_CONV, tf), gate),
            pl.BlockSpec((None, 1, tf), gate),
            pl.BlockSpec((None, dff, tn), down),
        ],
        out_specs=pl.BlockSpec((tm, tn), out),
        out_shape=jax.ShapeDtypeStruct((m, d), F32),
        scratch_shapes=[pltpu.VMEM((n_f, tm, tf), BF16), pltpu.VMEM((n_f, SUBLANES, tf), F32)],
        compiler_params=_params(("arbitrary", "arbitrary")),
        name="conv_ffn",
    )(xn, h, wg, wu, cw, cb.reshape(cb.shape[0], 1, dff), wd)


def _sb_kernel(*refs, tile, heads, scaled):
    q_ref, k_ref, v_ref = refs[:3]
    n_in = 3 + len(scaled) + sum(scaled)
    o_ref = refs[n_in]
    _convert_blocks(refs[3:n_in], refs[n_in + 1:], scaled)
    qi = pl.program_id(2)
    row = lax.broadcasted_iota(jnp.int32, (tile, tile), 0)
    col = lax.broadcasted_iota(jnp.int32, (tile, tile), 1)
    later = (row > col).astype(BF16)
    later2 = jnp.concatenate([later, later], axis=0)
    strictly_before = col < row

    def logits(q, hs, j, diagonal):
        k = k_ref[0, pl.ds(pl.multiple_of(j * tile, tile), tile), hs]
        z = lax.dot_general(q, k, (((1,), (1,)), ((), ())), preferred_element_type=F32)
        sp = jnp.maximum(z, 0.0) + jnp.log2(1.0 + jnp.exp2(-jnp.abs(z)))
        log_keep = -sp
        if diagonal:
            log_keep = jnp.where(strictly_before, log_keep, 0.0)
        hi = log_keep.astype(BF16)
        lo = (log_keep - hi.astype(F32)).astype(BF16)
        suffix = jnp.dot(jnp.concatenate([hi, lo], axis=1), later2, preferred_element_type=F32)
        return (z - sp) + suffix, suffix[:, :1] + log_keep[:, :1]

    def weighted(hs, j, w):
        v = v_ref[0, pl.ds(pl.multiple_of(j * tile, tile), tile), hs]
        return jnp.dot(w.astype(BF16), v, preferred_element_type=F32)

    jb = jnp.maximum(qi - 1, 0)
    no_left_tile = jnp.where(qi > 0, 0.0, NEG_BIG)
    head_slices = [slice(hd * HEAD_DIM, (hd + 1) * HEAD_DIM) for hd in range(heads)]
    first_two = []
    for hs in head_slices:
        q = q_ref[0, :, hs]
        la, ta = logits(q, hs, qi, True)
        lb, tb = logits(q, hs, jb, False)
        acc = weighted(hs, qi, jnp.where(strictly_before, jnp.exp2(la), 0.0))
        acc = acc + weighted(hs, jb, jnp.exp2(lb + (ta + no_left_tile)))
        first_two.append((q, ta + tb, acc))

    for hs, (q, run, acc) in zip(head_slices, first_two):

        def cond(carry):
            j, alive, _, _ = carry
            return jnp.logical_and(j >= 0, alive > 0)

        def body(carry, q=q, hs=hs):
            j, _, run, acc = carry
            lj, tj = logits(q, hs, j, False)
            acc = acc + weighted(hs, j, jnp.exp2(lj + run))
            run = run + tj
            alive = (jnp.max(run) > -SB_EXIT_LOG2).astype(jnp.int32)
            return j - 1, alive, run, acc

        alive0 = (jnp.max(run) > -SB_EXIT_LOG2).astype(jnp.int32)
        _, _, _, acc = lax.while_loop(cond, body, (qi - 2, alive0, run, acc))
        o_ref[0, :, hs] = acc.astype(o_ref.dtype)


def sb_attention(qkv, cast=(), tile=256, heads=4):
    b, t, _ = qkv.shape
    tile = min(tile, t)
    assert t % tile == 0 and SB_HEADS % heads == 0
    groups = SB_HEADS // heads
    width = heads * HEAD_DIM
    nq = t // tile
    steps = b * groups * nq
    in_specs = [
        pl.BlockSpec((1, tile, width), lambda bi, g, i: (bi, i, g)),
        pl.BlockSpec((1, t, width), lambda bi, g, i: (bi, 0, groups + g)),
        pl.BlockSpec((1, t, width), lambda bi, g, i: (bi, 0, 2 * groups + g)),
    ]
    c_in, c_args, c_out, c_shape, scaled = _cast_specs(cast, steps, lambda bi, g, i: (bi * groups + g) * nq + i)
    outs = pl.pallas_call(
        functools.partial(_sb_kernel, tile=tile, heads=heads, scaled=scaled),
        grid=(b, groups, nq),
        in_specs=in_specs + c_in,
        out_specs=[pl.BlockSpec((1, tile, width), lambda bi, g, i: (bi, i, g))] + c_out,
        out_shape=[jax.ShapeDtypeStruct((b, t, SB_WIDTH), BF16)] + c_shape,
        compiler_params=_params(("parallel", "parallel", "arbitrary")),
        name="sb_attention",
    )(qkv, qkv, qkv, *c_args)
    return (outs[0], *[o.reshape(w.shape) for o, (w, _) in zip(outs[1:], cast)])


SSD_INPUTS = 9


def _ssd_kernel(*refs, scaled):
    zxd_ref, halo_ref, cw_ref, cb_ref, dtb_ref, alog_ref, dskip_ref, nw_ref, expand_ref = refs[:SSD_INPUTS]
    n_in = SSD_INPUTS + len(scaled) + sum(scaled)
    o_ref, state_ref = refs[n_in], refs[-1]
    _convert_blocks(refs[SSD_INPUTS:n_in], refs[n_in + 1:-1], scaled)
    c = pl.program_id(1)
    L, N, P = SSM_CHUNK, SSM_STATE, SSM_HEAD_DIM
    heads_per_group = SSM_HEADS // SSM_GROUPS
    gw = heads_per_group * P

    @pl.when(c == 0)
    def _():
        state_ref[...] = jnp.zeros_like(state_ref)

    blk = zxd_ref[0]
    z = blk[:, :SSM_INNER]
    xbc = blk[:, SSM_INNER:SSM_INNER + CONV_DIM]
    dt_raw = blk[:, SSM_INNER + CONV_DIM:]
    halo = jnp.where(c == 0, 0.0, halo_ref[0][:, SSM_INNER:SSM_INNER + CONV_DIM])

    cw = cw_ref[...]
    conv = cw[3:4] * xbc + cb_ref[...]
    for s in range(1, SSM_CONV):
        conv = conv + cw[3 - s:4 - s] * _shift_rows(xbc, halo, s)
    xbc = conv * _sigmoid(conv)
    xs = xbc[:, :SSM_INNER]
    b_in = xbc[:, SSM_INNER:SSM_INNER + SSM_GROUPS * N]
    c_in = xbc[:, SSM_INNER + SSM_GROUPS * N:]

    pre = dt_raw + dtb_ref[...]
    dt = jnp.maximum(pre, 0.0) + jnp.log1p(jnp.exp(-jnp.abs(pre)))
    a = -jnp.exp(alog_ref[...])
    row = lax.broadcasted_iota(jnp.int32, (L, L), 0)
    col = lax.broadcasted_iota(jnp.int32, (L, L), 1)
    causal = row >= col
    tri = causal.astype(F32)
    a_cs = jnp.dot(tri, dt * a, preferred_element_type=F32, precision=lax.Precision.HIGHEST)
    a_cs_t = a_cs.T
    a_end = a_cs[L - 1:L, :]

    expand = expand_ref[...]
    spread = lambda v: jnp.dot(v.astype(BF16), expand[:LANES], preferred_element_type=F32)
    dt_l = spread(dt)
    decay_in_l = spread(jnp.exp(a_cs))
    decay_out_l = spread(jnp.exp(a_end - a_cs))
    cd = jnp.broadcast_to(jnp.exp(a_end), (SUBLANES, LANES))
    cd_hi = cd.astype(BF16)
    cd_r = cd - cd_hi.astype(F32)
    cd_mid = cd_r.astype(BF16)
    cd_lo = (cd_r - cd_mid.astype(F32)).astype(BF16)
    cd_l = jnp.dot(jnp.concatenate([cd_hi, cd_mid, cd_lo], axis=1), expand, preferred_element_type=F32)[:1]

    xdt = xs * dt_l
    xdt_bf = xdt.astype(BF16)
    xdt_out_bf = (xdt * decay_out_l).astype(BF16)

    lane = lax.broadcasted_iota(jnp.int32, (L, LANES), 1)
    first_half = lane < P
    zero = jnp.zeros((L, LANES), BF16)

    ys = []
    for g in range(SSM_GROUPS):
        gs = slice(g * gw, (g + 1) * gw)
        cg = c_in[:, g * N:(g + 1) * N].astype(BF16)
        bg = b_in[:, g * N:(g + 1) * N]
        scores = lax.dot_general(cg, bg.astype(BF16), (((1,), (1,)), ((), ())), preferred_element_type=F32)
        state = state_ref[g]
        y_off = jnp.dot(cg, state.astype(BF16), preferred_element_type=F32) * decay_in_l[:, gs]
        upd = jnp.dot(bg.T.astype(BF16), xdt_out_bf[:, gs], preferred_element_type=F32)
        state_ref[g] = state * cd_l[:, gs] + upd
        for jp in range(heads_per_group // 2):
            ps = slice(g * gw + jp * LANES, g * gw + (jp + 1) * LANES)
            sd = []
            for h in (g * heads_per_group + 2 * jp, g * heads_per_group + 2 * jp + 1):
                decay = jnp.where(causal, jnp.exp(a_cs[:, h:h + 1] - a_cs_t[h:h + 1, :]), 0.0)
                sd.append((scores * decay).astype(BF16))
            xp = xdt_bf[:, ps]
            rhs = jnp.concatenate([jnp.where(first_half, xp, zero), jnp.where(first_half, zero, xp)], axis=0)
            y = jnp.dot(jnp.concatenate(sd, axis=1), rhs, preferred_element_type=F32)
            ys.append(y + y_off[:, jp * LANES:(jp + 1) * LANES] + dskip_ref[:, ps] * xs[:, ps])

    y = jnp.concatenate(ys, axis=1) * (z * _sigmoid(z))
    o_ref[0] = _rms(y, nw_ref[...]).astype(o_ref.dtype)


def ssd(zxd, cw, cb, dt_bias, a_log, d_skip, norm_w, cast=()):
    b, t, width = zxd.shape
    L = SSM_CHUNK
    assert t % L == 0 and width == ZXD_WIDTH
    n_chunks = t // L
    c_in, c_args, c_out, c_shape, scaled = _cast_specs(cast, b * n_chunks, lambda bi, c: bi * n_chunks + c)
    pad = lambda p: jnp.pad(p.astype(F32), (0, DT_PAD - SSM_HEADS)).reshape(1, DT_PAD)
    dskip_lanes = jnp.repeat(d_skip.astype(F32), SSM_HEAD_DIM).reshape(1, SSM_INNER)
    expand = (jnp.arange(LANES)[:, None] == jnp.arange(SSM_INNER)[None, :] // SSM_HEAD_DIM).astype(BF16)
    expand = jnp.concatenate([expand] * 3, axis=0)
    full = lambda shape: pl.BlockSpec(shape, lambda bi, c: (0,) * len(shape))
    in_specs = [
        pl.BlockSpec((1, L, width), lambda bi, c: (bi, c, 0)),
        pl.BlockSpec((1, SUBLANES, width), lambda bi, c: (bi, jnp.maximum(c * (L // SUBLANES) - 1, 0), 0)),
        full((SSM_CONV, CONV_DIM)), full((1, CONV_DIM)), full((1, DT_PAD)), full((1, DT_PAD)),
        full((1, SSM_INNER)), full((1, SSM_INNER)), full((3 * LANES, SSM_INNER)),
    ]
    assert len(in_specs) == SSD_INPUTS
    outs = pl.pallas_call(
        functools.partial(_ssd_kernel, scaled=scaled),
        grid=(b, n_chunks),
        in_specs=in_specs + c_in,
        out_specs=[pl.BlockSpec((1, L, SSM_INNER), lambda bi, c: (bi, c, 0))] + c_out,
        out_shape=[jax.ShapeDtypeStruct((b, t, SSM_INNER), BF16)] + c_shape,
        scratch_shapes=[pltpu.VMEM((SSM_GROUPS, SSM_STATE, (SSM_HEADS // SSM_GROUPS) * SSM_HEAD_DIM), F32)],
        compiler_params=_params(("parallel", "arbitrary")),
        name="ssd",
    )(zxd, zxd, cw, cb.reshape(1, CONV_DIM), pad(dt_bias), pad(a_log), dskip_lanes, norm_w.reshape(1, SSM_INNER),
      expand, *c_args)
    return (outs[0], *[o.reshape(w.shape) for o, (w, _) in zip(outs[1:], cast)])


def _dilated_kernel(q_ref, k_ref, v_ref, o_ref, xf, qd, kd_all, vd_all, bias, of, lf, *, span, heads):
    si = pl.program_id(2)
    blk = DIL_BLOCK
    units = span // blk
    cur = si % 2
    prv = 1 - cur

    @pl.when(si == 0)
    def _():
        kd_all[:, :, 1] = jnp.zeros(kd_all.shape[:2] + kd_all.shape[3:], kd_all.dtype)
        vd_all[:, :, 1] = jnp.zeros(vd_all.shape[:2] + vd_all.shape[3:], vd_all.dtype)
        qpos = lax.broadcasted_iota(jnp.int32, (blk, 2 * blk), 0)
        kpos = lax.broadcasted_iota(jnp.int32, (blk, 2 * blk), 1)
        dist = qpos + blk - kpos
        for bi, (window, r) in enumerate(DIL_PATTERNS):
            band = jnp.logical_and(dist >= 0, dist <= window // r)
            bias[bi, 0] = jnp.where(band, 0.0, NEG_BIG)
            bias[bi, 1] = jnp.where(jnp.logical_and(band, kpos >= blk), 0.0, NEG_BIG)

    first = jnp.where(si == 0, 1, 0)
    ones = jnp.ones((2 * blk, HEAD_DIM), BF16)
    for hd in range(heads):
        hs = pl.ds(hd * HEAD_DIM, HEAD_DIM)
        _dilated_head(q_ref.at[0, :, hs], k_ref.at[0, :, hs], v_ref.at[0, :, hs], o_ref.at[0, :, hs], xf, qd,
                      kd_all.at[hd], vd_all.at[hd], bias, of, lf, span, cur, prv, first, ones)


def _dilated_head(q_ref, k_ref, v_ref, o_ref, xf, qd, kd, vd, bias, of, lf, span, cur, prv, first, ones):
    blk = DIL_BLOCK
    units = span // blk
    xf[0] = q_ref[...].astype(F32)
    xf[1] = k_ref[...].astype(F32)
    xf[2] = v_ref[...].astype(F32)
    for bi, (_, r) in enumerate(DIL_PATTERNS):
        rows = span // r
        if r == 1:
            kd[bi, cur] = k_ref[...]
            vd[bi, cur] = v_ref[...]
            continue
        for c in range(r):
            src = pl.ds(c, rows, stride=r)
            dst = pl.ds(c * rows, rows)
            qd[bi, dst, :] = xf[0, src, :].astype(BF16)
            kd[bi, cur, dst, :] = xf[1, src, :].astype(BF16)
            vd[bi, cur, dst, :] = xf[2, src, :].astype(BF16)

    for bi, (_, r) in enumerate(DIL_PATTERNS):
        rows = span // r
        for u in range(units):
            cls, nb = divmod(u, units // r)
            base = cls * rows + nb * blk
            here = pl.ds(base, blk)
            q = q_ref[here, :] if r == 1 else qd[bi, here, :]
            if nb > 0:
                before = pl.ds(base - blk, blk)
                k_prev, v_prev = kd[bi, cur, before, :], vd[bi, cur, before, :]
                b = bias[bi, 0]
            else:
                before = pl.ds(cls * rows + rows - blk, blk)
                k_prev, v_prev = kd[bi, prv, before, :], vd[bi, prv, before, :]
                b = bias[bi, first]
            k = jnp.concatenate([k_prev, kd[bi, cur, here, :]], axis=0)
            v = jnp.concatenate([v_prev, vd[bi, cur, here, :]], axis=0)
            s = lax.dot_general(q, k, (((1,), (1,)), ((), ())), preferred_element_type=F32) + b
            m = jnp.max(s, axis=-1, keepdims=True)
            p = jnp.exp2(s - m).astype(BF16)
            pv = jnp.dot(p, jnp.concatenate([v, ones], axis=1), preferred_element_type=F32)
            den = pv[:, HEAD_DIM:]
            o = pv[:, :HEAD_DIM] / den
            lse = m + jnp.log2(den)
            natural = pl.ds(base, blk) if r == 1 else pl.ds(cls + r * blk * nb, blk, stride=r)
            of[bi, natural, :] = o
            lf[bi, natural, :] = lse

    l0, l1, l2 = lf[0], lf[1], lf[2]
    m = jnp.maximum(jnp.maximum(l0, l1), l2)
    e0, e1, e2 = jnp.exp2(l0 - m), jnp.exp2(l1 - m), jnp.exp2(l2 - m)
    out = (e0 * of[0] + e1 * of[1] + e2 * of[2]) / (e0 + e1 + e2)
    o_ref[...] = out.astype(o_ref.dtype)


def dilated_attention(qkv, heads=2):
    b, t, _ = qkv.shape
    span = DIL_BLOCK * max(r for _, r in DIL_PATTERNS)
    assert t % span == 0 and len(DIL_PATTERNS) == 3 and DIL_HEADS % heads == 0
    nh, nbr = DIL_HEADS, len(DIL_PATTERNS)
    groups = nh // heads
    width = heads * HEAD_DIM
    spec = lambda off: pl.BlockSpec((1, span, width), lambda bi, g, s: (bi, s, off + g))
    return pl.pallas_call(
        functools.partial(_dilated_kernel, span=span, heads=heads),
        grid=(b, groups, t // span),
        in_specs=[spec(0), spec(groups), spec(2 * groups)],
        out_specs=pl.BlockSpec((1, span, width), lambda bi, g, s: (bi, s, g)),
        out_shape=jax.ShapeDtypeStruct((b, t, nh * HEAD_DIM), BF16),
        scratch_shapes=[
            pltpu.VMEM((3, span, HEAD_DIM), F32),
            pltpu.VMEM((nbr, span, HEAD_DIM), BF16),
            pltpu.VMEM((heads, nbr, 2, span, HEAD_DIM), BF16),
            pltpu.VMEM((heads, nbr, 2, span, HEAD_DIM), BF16),
            pltpu.VMEM((nbr, 2, DIL_BLOCK, 2 * DIL_BLOCK), F32),
            pltpu.VMEM((nbr, span, HEAD_DIM), F32),
            pltpu.VMEM((nbr, span, HEAD_DIM), F32),
        ],
        compiler_params=_params(("parallel", "parallel", "arbitrary")),
        name="dilated_attention",
    )(qkv, qkv, qkv)


def _rmsnorm_kernel(x_ref, w_ref, o_ref):
    o_ref[...] = _rms(x_ref[...], w_ref[...])


def rmsnorm(x, w, tm=512):
    m, d = x.shape
    tm = min(tm, m)
    assert m % tm == 0
    return pl.pallas_call(
        _rmsnorm_kernel,
        grid=(m // tm,),
        in_specs=[pl.BlockSpec((tm, d), lambda i: (i, 0)), pl.BlockSpec((1, d), lambda i: (0, 0))],
        out_specs=pl.BlockSpec((tm, d), lambda i: (i, 0)),
        out_shape=jax.ShapeDtypeStruct((m, d), F32),
        compiler_params=_params(("parallel",)),
        name="rmsnorm",
    )(x, w.reshape(1, d))


def _q_colscale(n_cols, q_cols):
    return jnp.where(jnp.arange(n_cols) < q_cols, LOG2_E * HEAD_DIM ** -0.5, 1.0).astype(F32)


def _even_mixer(h, bsz, seq, i, nw, ffn_nw, w_qkv_t, w_zxd_t, conv_w, conv_b, dt_bias, a_log, d_skip, ssm_norm_w, w_out,
                ssd_cast=(), sb_cast=()):
    zxd = norm_matmul(h, nw, w_zxd_t, i, F32, transposed=True, tn=ZXD_WIDTH // 3)
    o_b, *ssd_bf16s = ssd(zxd.reshape(bsz, seq, ZXD_WIDTH), conv_w, conv_b, dt_bias, a_log, d_skip, ssm_norm_w,
                          ssd_cast)
    qkv = norm_matmul(h, nw, w_qkv_t, i, BF16, transposed=True)
    o_a, *sb_bf16s = sb_attention(qkv.reshape(bsz, seq, 3 * SB_WIDTH), sb_cast)
    if w_out is None:
        w_out = sb_bf16s[0]
    h, xn = matmul_residual([o_a.reshape(bsz * seq, SB_WIDTH), o_b.reshape(bsz * seq, SSM_INNER)], w_out, i, h,
                            ffn_nw)
    return h, xn, ssd_bf16s, sb_bf16s


def _odd_mixer(h, bsz, seq, i, nw, ffn_nw, w_qkv, w_out):
    qkv = norm_matmul(h, nw, w_qkv, i, BF16)
    o = dilated_attention(qkv.reshape(bsz, seq, 3 * D_MODEL)).reshape(bsz * seq, D_MODEL)
    return matmul_residual([o], w_out, i, h, ffn_nw)


def kernel(x, mix_norm_w, ffn_norm_w, final_norm_w, ev_w_in, ev_conv_w, ev_conv_b, ev_dt_bias, ev_a_log, ev_d_skip, ev_ssm_norm_w, ev_w_out, od_w_in, od_w_out, ffn_w_gate, ffn_w_up, ffn_conv_w, ffn_conv_b, ffn_w_down):
    bsz, seq, d = x.shape
    ev_qkv, ev_zxd = split_in_proj(jnp.swapaxes(ev_w_in, 1, 2), LOG2_E * HEAD_DIM ** -0.5)
    ssd_casts = ((ffn_w_down, None),)
    sb_casts = ((ev_w_out, None), (od_w_in, _q_colscale(od_w_in.shape[-1], D_MODEL)), (od_w_out, None),
                (ffn_w_gate, None), (ffn_w_up, None))
    h = x.reshape(bsz * seq, d)
    for layer in range(DEPTH):
        i = layer // 2
        if layer % 2 == 0:
            first = layer == 0
            h, xn, from_ssd, from_sb = _even_mixer(
                h, bsz, seq, i, mix_norm_w[layer], ffn_norm_w[layer], ev_qkv, ev_zxd, ev_conv_w[i], ev_conv_b[i],
                ev_dt_bias[i], ev_a_log[i], ev_d_skip[i], ev_ssm_norm_w[i], None if first else ev_out,
                ssd_cast=ssd_casts if first else (), sb_cast=sb_casts if first else ())
            if first:
                (w_down,), (ev_out, od_qkv, od_out, w_gate, w_up) = from_ssd, from_sb
        else:
            h, xn = _odd_mixer(h, bsz, seq, i, mix_norm_w[layer], ffn_norm_w[layer], od_qkv, od_out)
        h = conv_ffn(xn, h, seq, w_gate, w_up, ffn_conv_w, ffn_conv_b, w_down, layer)
    return rmsnorm(h, final_norm_w).reshape(bsz, seq, d)
```
